```python
import math
import jax, jax.numpy as jnp
from jax import lax
import numpy as np

D_MODEL = 1024
BATCH = 4
SEQ = 4096
DEPTH = 4

DA_HEADS = 4
DA_QK_DIM = 64
DA_V_DIM = 2 * DA_QK_DIM
DA_QK_WIDTH = DA_HEADS * 2 * DA_QK_DIM
DA_WIDTH = DA_HEADS * DA_V_DIM
ROPE_THETA = 500000.0
ROT_DIM = DA_QK_DIM // 4
Q_BLOCK = 128
HG_HEADS = 4
HG_K_DIM = 128
HG_V_DIM = 128
HG_K_WIDTH = HG_HEADS * HG_K_DIM
HG_V_WIDTH = HG_HEADS * HG_V_DIM
HG_CHUNK = 64
FFN_HIDDEN = -(-8 * D_MODEL // (3 * 256)) * 256
NORM_EPS = 1e-6
IN_SIZES = (DA_QK_WIDTH, DA_QK_WIDTH, DA_WIDTH,
            HG_K_WIDTH, HG_K_WIDTH, HG_K_WIDTH, HG_V_WIDTH, HG_V_WIDTH,
            D_MODEL, D_MODEL)
IN_WIDTH = sum(IN_SIZES)

kernel_name = "hybrid_diffattn_hgrn2_gated_encoder"


def rmsnorm(x, gain):
    xf = x.astype(jnp.float32)
    y = xf * lax.rsqrt(jnp.mean(xf * xf, axis=-1, keepdims=True) + NORM_EPS)
    return (y * gain.astype(jnp.float32)).astype(x.dtype)


def split_columns(p):
    outs, start = [], 0
    for size in IN_SIZES:
        outs.append(p[..., start:start + size])
        start += size
    return outs


def rope_tables(positions):
    inv_freq = ROPE_THETA ** (-(jnp.arange(0, ROT_DIM, 2, dtype=jnp.float32) / ROT_DIM))
    ang = positions.astype(jnp.float32)[..., None] * inv_freq
    return jnp.cos(ang), jnp.sin(ang)


def apply_partial_rope(t, cos, sin):
    c = cos[:, :, None, None, :].astype(t.dtype)
    s = sin[:, :, None, None, :].astype(t.dtype)
    half = ROT_DIM // 2
    t1 = t[..., :half]
    t2 = t[..., half:ROT_DIM]
    rot = jnp.concatenate([t1 * c - t2 * s, t2 * c + t1 * s], axis=-1)
    return jnp.concatenate([rot, t[..., ROT_DIM:]], axis=-1)


def diff_attention(h_q, h_k, h_v, cos, sin, lam, norm_gain, layer):
    B, S, _ = h_q.shape
    q = h_q.reshape(B, S, DA_HEADS, 2, DA_QK_DIM)
    k = h_k.reshape(B, S, DA_HEADS, 2, DA_QK_DIM)
    v = h_v.reshape(B, S, DA_HEADS, DA_V_DIM)
    q = apply_partial_rope(q, cos, sin) * (DA_QK_DIM ** -0.5)
    k = apply_partial_rope(k, cos, sin)
    lam_init = 0.8 - 0.6 * math.exp(-0.3 * layer)
    l32 = lam.astype(jnp.float32)
    lam_full = (jnp.exp(jnp.sum(l32[0] * l32[1])) - jnp.exp(jnp.sum(l32[2] * l32[3]))
                + lam_init)
    nq = S // Q_BLOCK
    qb = q.reshape(B, nq, Q_BLOCK, DA_HEADS, 2, DA_QK_DIM).transpose(1, 0, 2, 3, 4, 5)

    def block(qi):
        s = jnp.einsum('bqhcd,bkhcd->bhcqk', qi, k).astype(jnp.float32)
        p = jax.nn.softmax(s, axis=-1)
        w = p[:, :, 0] - lam_full * p[:, :, 1]
        return jnp.einsum('bhqk,bkhv->bqhv', w.astype(v.dtype), v)

    o = lax.map(block, qb)
    o = o.transpose(1, 0, 2, 3, 4).reshape(B, S, DA_HEADS, DA_V_DIM)
    o = rmsnorm(o, norm_gain.reshape(DA_HEADS, DA_V_DIM)) * (1.0 - lam_init)
    return o.reshape(B, S, DA_WIDTH)


def hgrn_lower_bounds(lb_logits):
    p = jax.nn.softmax(lb_logits.astype(jnp.float32), axis=1)
    c = jnp.cumsum(p, axis=1)
    return c - c[:, :1]


def log_forget(z, lb):
    return jnp.logaddexp(jnp.log(lb), jnp.log1p(-lb) + jax.nn.log_sigmoid(z))


def chunk_scan(q, k, g, v):
    B, S, H, dk = q.shape
    dv = v.shape[-1]
    n = S // HG_CHUNK

    def to_chunks(t):
        return t.reshape(B, n, HG_CHUNK, H, t.shape[-1]).transpose(1, 0, 3, 2, 4)

    tril = jnp.tril(jnp.ones((HG_CHUNK, HG_CHUNK), dtype=bool))

    def step(state, inp):
        qc, kc, gc, vc = inp
        b = jnp.cumsum(gc, axis=2)
        inter = jnp.einsum('bhck,bhkv->bhcv', qc * jnp.exp(b), state)
        diff = b[:, :, :, None, :] - b[:, :, None, :, :]
        decay = jnp.exp(jnp.where(tril[:, :, None], diff, -jnp.inf))
        scores = jnp.einsum('bhtk,bhsk,bhtsk->bhts', qc, kc, decay)
        intra = jnp.einsum('bhts,bhsv->bhtv', scores, vc)
        b_last = b[:, :, -1, :]
        state = (jnp.exp(b_last)[..., None] * state
                 + jnp.einsum('bhck,bhcv->bhkv', kc * jnp.exp(b_last[:, :, None, :] - b), vc))
        return state, inter + intra

    s0 = jnp.zeros((B, H, dk, dv), jnp.float32)
    _, out = lax.scan(step, s0, (to_chunks(q), to_chunks(k), to_chunks(g), to_chunks(v)))
    return out.transpose(1, 0, 3, 2, 4).reshape(B, S, H, dv)


def hgrn2_bidirectional(h_q, h_ff, h_fb, h_i, h_g, lb_f, lb_b, norm_gain):
    B, S, _ = h_q.shape
    f32 = jnp.float32
    q = h_q.astype(f32).reshape(B, S, HG_HEADS, HG_K_DIM)
    v = h_i.astype(f32).reshape(B, S, HG_HEADS, HG_V_DIM)
    g_f = log_forget(h_ff.astype(f32).reshape(B, S, HG_HEADS, HG_K_DIM),
                     lb_f.reshape(HG_HEADS, HG_K_DIM))
    g_b = log_forget(h_fb.astype(f32).reshape(B, S, HG_HEADS, HG_K_DIM),
                     lb_b.reshape(HG_HEADS, HG_K_DIM))
    k_f = -jnp.expm1(g_f)
    k_b = -jnp.expm1(g_b)
    out_f = chunk_scan(q, k_f, g_f, v)
    out_b = chunk_scan(q[:, ::-1], k_b[:, ::-1], g_b[:, ::-1], v[:, ::-1])[:, ::-1]
    o = rmsnorm(out_f + out_b, norm_gain.reshape(HG_HEADS, HG_V_DIM))
    o = o * jax.nn.sigmoid(h_g.astype(f32).reshape(B, S, HG_HEADS, HG_V_DIM))
    return o.reshape(B, S, HG_V_WIDTH).astype(h_q.dtype)


def setup_inputs(seed: int = 0) -> dict:
    key = jax.random.key(seed)
    ks = jax.random.split(key, 18)
    nrm = jax.random.normal
    f32 = jnp.float32
    x = nrm(ks[0], (BATCH, SEQ, D_MODEL), f32)
    offset = jax.random.randint(ks[1], (BATCH, 1), 0, 1024, dtype=jnp.int32)
    positions = offset + jnp.arange(SEQ, dtype=jnp.int32)[None, :]
    w_in = nrm(ks[2], (DEPTH, D_MODEL, IN_WIDTH), f32) * D_MODEL ** -0.5
    da_lambda = nrm(ks[3], (DEPTH, 4, DA_QK_DIM), f32) * 0.1
    da_norm = 1.0 + 0.02 * nrm(ks[4], (DEPTH, DA_WIDTH), f32)
    hg_lb_logits = 0.5 * nrm(ks[5], (2, DEPTH, HG_K_WIDTH), f32)
    hg_norm = 1.0 + 0.02 * nrm(ks[6], (DEPTH, HG_V_WIDTH), f32)
    w_a = nrm(ks[7], (DEPTH, DA_WIDTH, D_MODEL), f32) * DA_WIDTH ** -0.5
    w_b = nrm(ks[8], (DEPTH, HG_V_WIDTH, D_MODEL), f32) * HG_V_WIDTH ** -0.5
    w_o = nrm(ks[9], (DEPTH, D_MODEL, D_MODEL), f32) * D_MODEL ** -0.5
    attn_norm = 1.0 + 0.02 * nrm(ks[10], (DEPTH, D_MODEL), f32)
    ffn_norm = 1.0 + 0.02 * nrm(ks[11], (DEPTH, D_MODEL), f32)
    w_gate = nrm(ks[12], (DEPTH, D_MODEL, FFN_HIDDEN), f32) * D_MODEL ** -0.5
    w_up = nrm(ks[13], (DEPTH, D_MODEL, FFN_HIDDEN), f32) * D_MODEL ** -0.5
    w_down = nrm(ks[14], (DEPTH, FFN_HIDDEN, D_MODEL), f32) * FFN_HIDDEN ** -0.5
    final_norm = 1.0 + 0.02 * nrm(ks[15], (D_MODEL,), f32)
    return {"x": x, "positions": positions, "w_in": w_in, "da_lambda": da_lambda,
            "da_norm": da_norm, "hg_lb_logits": hg_lb_logits, "hg_norm": hg_norm,
            "w_a": w_a, "w_b": w_b, "w_o": w_o, "attn_norm": attn_norm,
            "ffn_norm": ffn_norm, "w_gate": w_gate, "w_up": w_up, "w_down": w_down,
            "final_norm": final_norm}


def reference(x, positions, w_in, da_lambda, da_norm, hg_lb_logits, hg_norm, w_a, w_b,
              w_o, attn_norm, ffn_norm, w_gate, w_up, w_down, final_norm):
    cos, sin = rope_tables(positions)
    lbs = hgrn_lower_bounds(hg_lb_logits)
    for layer in range(DEPTH):
        h = rmsnorm(x, attn_norm[layer])
        proj = h @ w_in[layer]
        (a_q, a_k, a_v, b_q, b_ff, b_fb, b_i, b_g, gate_a, gate_b) = split_columns(proj)
        y_a = diff_attention(a_q, a_k, a_v, cos, sin, da_lambda[layer], da_norm[layer],
                             layer) @ w_a[layer]
        y_b = hgrn2_bidirectional(b_q, b_ff, b_fb, b_i, b_g,
                                  lbs[0, layer].astype(x.dtype).astype(jnp.float32),
                                  lbs[1, layer].astype(x.dtype).astype(jnp.float32),
                                  hg_norm[layer]) @ w_b[layer]
        merged = jax.nn.sigmoid(gate_a) * y_a + jax.nn.sigmoid(gate_b) * y_b
        x = x + merged @ w_o[layer]
        h = rmsnorm(x, ffn_norm[layer])
        x = x + (jax.nn.silu(h @ w_gate[layer]) * (h @ w_up[layer])) @ w_down[layer]
    return rmsnorm(x, final_norm)
```

```python
import functools
import math

import numpy as np
import jax
import jax.numpy as jnp
from jax import lax
from jax.experimental import pallas as pl
from jax.experimental.pallas import tpu as pltpu

F32 = jnp.float32
BF16 = jnp.bfloat16

D_MODEL = 1024
N_HEADS = 4
HEAD_W = 128
QK_DIM = 64
ROT_DIM = 16
ROPE_THETA = 500000.0
FFN_HIDDEN = 2816
NORM_EPS = 1e-6
MIX_W = N_HEADS * HEAD_W
QKV_W = 3 * MIX_W
HG_W = 5 * MIX_W
GATE_W = 2 * D_MODEL
IN_WIDTH = QKV_W + HG_W + GATE_W

LANES = 128
MXU_N = 256
VMEM_LIMIT = 56 * 1024 * 1024

HG_CHUNK = 128
HG_LEVELS = HG_CHUNK.bit_length() - 1

NT_DIMS = (((1,), (1,)), ((), ()))


def _cparams(sem):
    return pltpu.CompilerParams(dimension_semantics=sem, vmem_limit_bytes=VMEM_LIMIT)


def _sigmoid(x):
    return 1.0 / (1.0 + jnp.exp(-x))


def _rms_scale(x, gain):
    ms = jnp.mean(x * x, axis=-1, keepdims=True)
    return x * lax.rsqrt(ms + NORM_EPS) * gain


def _rmsnorm_kernel(x_ref, g_ref, o_ref):
    o_ref[...] = _rms_scale(x_ref[...], g_ref[...]).astype(o_ref.dtype)


def _rmsnorm(x2d, gain, tm):
    m, d = x2d.shape
    return pl.pallas_call(
        _rmsnorm_kernel,
        out_shape=jax.ShapeDtypeStruct((m, d), BF16),
        grid=(m // tm,),
        in_specs=[pl.BlockSpec((tm, d), lambda i: (i, 0)),
                  pl.BlockSpec((1, d), lambda i: (0, 0))],
        out_specs=pl.BlockSpec((tm, d), lambda i: (i, 0)),
        compiler_params=_cparams(("parallel",)),
        name="rmsnorm0",
    )(x2d, gain.reshape(1, d))


def _proj_kernel(h_ref, w_ref, c_ref, sa_ref, sb_ref, qkv_ref, hg_ref, gate_ref):
    h = h_ref[...]
    cos_t = c_ref[...]
    sin_a = sa_ref[...]
    sin_b = sb_ref[...]

    def rope(a, scale):
        outs = []
        for hh in range(N_HEADS):
            t = a[:, hh * HEAD_W:(hh + 1) * HEAD_W]
            r = (t * cos_t + pltpu.roll(t, HEAD_W - ROT_DIM // 2, 1) * sin_a
                 + pltpu.roll(t, ROT_DIM // 2, 1) * sin_b)
            outs.append(r * scale if scale != 1.0 else r)
        return jnp.concatenate(outs, axis=1)

    n_groups = IN_WIDTH // MIX_W
    for g in range(n_groups):
        acc = jnp.dot(h, w_ref[:, g * MIX_W:(g + 1) * MIX_W], preferred_element_type=F32)
        if g == 0:
            qkv_ref[:, 0:MIX_W] = rope(acc, QK_DIM ** -0.5).astype(BF16)
        elif g == 1:
            qkv_ref[:, MIX_W:2 * MIX_W] = rope(acc, 1.0).astype(BF16)
        elif g == 2:
            qkv_ref[:, 2 * MIX_W:3 * MIX_W] = acc.astype(BF16)
        elif g < 8:
            hg_ref[:, (g - 3) * MIX_W:(g - 2) * MIX_W] = acc
        else:
            gate_ref[:, (g - 8) * MIX_W:(g - 7) * MIX_W] = _sigmoid(acc).astype(BF16)


def _in_proj(h, w, cos_t, sin_a, sin_b, tm):
    m = h.shape[0]
    row = lambda i: (i, 0)
    fixed = lambda i: (0, 0)
    return pl.pallas_call(
        _proj_kernel,
        out_shape=(jax.ShapeDtypeStruct((m, QKV_W), BF16),
                   jax.ShapeDtypeStruct((m, HG_W), F32),
                   jax.ShapeDtypeStruct((m, GATE_W), BF16)),
        grid=(m // tm,),
        in_specs=[pl.BlockSpec((tm, D_MODEL), row),
                  pl.BlockSpec((D_MODEL, IN_WIDTH), fixed),
                  pl.BlockSpec((tm, HEAD_W), row),
                  pl.BlockSpec((tm, HEAD_W), row),
                  pl.BlockSpec((tm, HEAD_W), row)],
        out_specs=(pl.BlockSpec((tm, QKV_W), row),
                   pl.BlockSpec((tm, HG_W), row),
                   pl.BlockSpec((tm, GATE_W), row)),
        compiler_params=_cparams(("parallel",)),
        name="in_proj",
    )(h, w, cos_t, sin_a, sin_b)


def _attn_kernel(lam_ref, q_ref, k_ref, v_ref, g_ref, o_ref, s_scr, m_scr, acc_scr,
                 *, tq, tk, out_scale):
    seq = k_ref.shape[1]
    nk = seq // tk
    q = q_ref[0]
    lane = lax.broadcasted_iota(jnp.int32, (tq, HEAD_W), 1)
    zero = jnp.zeros_like(q)
    qs = jnp.concatenate([jnp.where(lane < QK_DIM, q, zero),
                          jnp.where(lane >= QK_DIM, q, zero)], axis=0)

    m_scr[...] = jnp.full(m_scr.shape, -jnp.inf, F32)

    def scores(j, carry):
        off = pl.multiple_of(j * tk, tk)
        s = lax.dot_general(qs, k_ref[0, pl.ds(off, tk), :], NT_DIMS,
                            preferred_element_type=F32)
        s_scr[j] = s
        mx = s[:, 0:LANES]
        for c in range(1, tk // LANES):
            mx = jnp.maximum(mx, s[:, c * LANES:(c + 1) * LANES])
        m_scr[...] = jnp.maximum(m_scr[...], mx)
        return carry

    lax.fori_loop(0, nk, scores, 0)

    m_row = jnp.max(m_scr[...], axis=1, keepdims=True)
    m_scr[...] = jnp.broadcast_to(m_row, m_scr.shape)
    acc_scr[...] = jnp.zeros(acc_scr.shape, F32)

    def weighted(j, carry):
        off = pl.multiple_of(j * tk, tk)
        m_b = m_scr[...]
        s = s_scr[j]
        p = jnp.concatenate(
            [jnp.exp(s[:, c * LANES:(c + 1) * LANES] - m_b) for c in range(tk // LANES)],
            axis=1).astype(BF16)
        vc = v_ref[0, pl.ds(off, tk), :]
        v_ext = jnp.concatenate([vc, jnp.ones_like(vc)], axis=1)
        acc_scr[...] += jnp.dot(p, v_ext, preferred_element_type=F32)
        return carry

    lax.fori_loop(0, nk, weighted, 0)

    acc = acc_scr[...]
    o1 = acc[:tq, :HEAD_W] / acc[:tq, HEAD_W:]
    o2 = acc[tq:, :HEAD_W] / acc[tq:, HEAD_W:]
    o = o1 - lam_ref[...] * o2
    o_ref[0] = (_rms_scale(o, g_ref[0]) * out_scale).astype(o_ref.dtype)


def _diff_attention(qkv, lam, gain, batch, seq, tq, tk, out_scale):
    qkv3 = qkv.reshape(batch, seq, QKV_W)
    kern = functools.partial(_attn_kernel, tq=tq, tk=tk, out_scale=out_scale)
    return pl.pallas_call(
        kern,
        out_shape=jax.ShapeDtypeStruct((batch, seq, MIX_W), BF16),
        grid=(batch, N_HEADS, seq // tq),
        in_specs=[pl.BlockSpec((1, HEAD_W), lambda b, h, i: (0, 0)),
                  pl.BlockSpec((1, tq, HEAD_W), lambda b, h, i: (b, i, h)),
                  pl.BlockSpec((1, seq, HEAD_W), lambda b, h, i: (b, 0, N_HEADS + h)),
                  pl.BlockSpec((1, seq, HEAD_W), lambda b, h, i: (b, 0, 2 * N_HEADS + h)),
                  pl.BlockSpec((1, 1, HEAD_W), lambda b, h, i: (h, 0, 0))],
        out_specs=pl.BlockSpec((1, tq, HEAD_W), lambda b, h, i: (b, i, h)),
        scratch_shapes=[pltpu.VMEM((seq // tk, 2 * tq, tk), F32),
                        pltpu.VMEM((2 * tq, LANES), F32),
                        pltpu.VMEM((2 * tq, 2 * HEAD_W), F32)],
        compiler_params=_cparams(("parallel", "parallel", "arbitrary")),
        name="diff_attn",
    )(lam, qkv3, qkv3, qkv3, gain.reshape(N_HEADS, 1, HEAD_W))


def _hg_masks():
    c = HG_CHUNK
    t = np.arange(c)[:, None]
    s = np.arange(c)[None, :]
    fwd = []
    for lvl in range(HG_LEVELS):
        h = 1 << lvl
        same = (t // (2 * h)) == (s // (2 * h))
        fwd.append(same & ((t & h) != 0) & ((s & h) == 0))
    bwd = [m.T for m in fwd]
    eye = t == s
    return np.stack(fwd + bwd + [eye]).astype(np.float32)


def _hg_chunk(q, z, v, st_t, lb, mask_ref, rev):
    c = HG_CHUNK
    oml = 1.0 - lb
    ez = jnp.exp(-jnp.abs(z))
    r = 1.0 / (1.0 + ez)
    er = ez * r
    pos = z >= 0.0
    f = lb + oml * jnp.where(pos, r, er)
    k = oml * jnp.where(pos, er, r)
    row = lax.broadcasted_iota(jnp.int32, (c, HEAD_W), 0)
    a_in = f
    e_ex = jnp.ones_like(f)
    tot = f
    sc = jnp.sum(q * k, axis=1, keepdims=True) * mask_ref[2 * HG_LEVELS]
    for lvl in range(HG_LEVELS):
        h = 1 << lvl
        bit = (row & h) != 0
        qside = jnp.logical_not(bit) if rev else bit
        x = jnp.where(qside, q * a_in, k * e_ex).astype(BF16)
        lv = lax.dot_general(x, x, NT_DIMS, preferred_element_type=F32)
        sc = sc + lv * mask_ref[(HG_LEVELS if rev else 0) + lvl]
        prev = pltpu.roll(tot, h, 0)
        nxt = pltpu.roll(tot, c - h, 0)
        if rev:
            a_in = a_in * jnp.where(bit, 1.0, nxt)
            e_ex = e_ex * jnp.where(bit, prev, 1.0)
        else:
            a_in = a_in * jnp.where(bit, prev, 1.0)
            e_ex = e_ex * jnp.where(bit, 1.0, nxt)
        tot = tot * jnp.where(bit, prev, nxt)
    vb = v.astype(BF16)
    intra = jnp.dot(sc.astype(BF16), vb, preferred_element_type=F32)
    inter = lax.dot_general((q * a_in).astype(BF16), st_t.astype(BF16), NT_DIMS,
                            preferred_element_type=F32)
    upd = jnp.dot(v.T.astype(BF16), (k * e_ex).astype(BF16), preferred_element_type=F32)
    st_t = st_t * tot[0:1, :] + upd
    return inter + intra, st_t


def _hgrn_kernel(q_ref, zf_ref, zb_ref, v_ref, g_ref, lb_ref, gain_ref, mask_ref, o_ref,
                 of_scr, ob_scr, *, tr):
    seq = q_ref.shape[1]
    c = HG_CHUNK
    n = seq // c
    lb_f = lb_ref[0, 0:1, :]
    lb_b = lb_ref[0, 1:2, :]

    def body(i, carry):
        st_f, st_b = carry
        cf = pl.multiple_of(i * c, c)
        cb = pl.multiple_of((n - 1 - i) * c, c)
        o_f, st_f = _hg_chunk(q_ref[0, pl.ds(cf, c), :], zf_ref[0, pl.ds(cf, c), :],
                              v_ref[0, pl.ds(cf, c), :], st_f, lb_f, mask_ref, False)
        of_scr[pl.ds(cf, c), :] = o_f
        o_b, st_b = _hg_chunk(q_ref[0, pl.ds(cb, c), :], zb_ref[0, pl.ds(cb, c), :],
                              v_ref[0, pl.ds(cb, c), :], st_b, lb_b, mask_ref, True)
        ob_scr[pl.ds(cb, c), :] = o_b
        return st_f, st_b

    zero = jnp.zeros((HEAD_W, HEAD_W), F32)
    lax.fori_loop(0, n, body, (zero, zero))

    gain = gain_ref[0]

    def finish(i, carry):
        off = pl.multiple_of(i * tr, tr)
        o = of_scr[pl.ds(off, tr), :] + ob_scr[pl.ds(off, tr), :]
        y = _rms_scale(o, gain) * _sigmoid(g_ref[0, pl.ds(off, tr), :])
        o_ref[0, pl.ds(off, tr), :] = y.astype(o_ref.dtype)
        return carry

    lax.fori_loop(0, seq // tr, finish, 0)


def _hgrn2(hg, lbs, gain, batch, seq, tr):
    hg3 = hg.reshape(batch, seq, HG_W)
    masks = jnp.asarray(_hg_masks())
    col = lambda g: (lambda b, h: (b, 0, g * N_HEADS + h))
    kern = functools.partial(_hgrn_kernel, tr=tr)
    return pl.pallas_call(
        kern,
        out_shape=jax.ShapeDtypeStruct((batch, seq, MIX_W), BF16),
        grid=(batch, N_HEADS),
        in_specs=[pl.BlockSpec((1, seq, HEAD_W), col(0)),
                  pl.BlockSpec((1, seq, HEAD_W), col(1)),
                  pl.BlockSpec((1, seq, HEAD_W), col(2)),
                  pl.BlockSpec((1, seq, HEAD_W), col(3)),
                  pl.BlockSpec((1, seq, HEAD_W), col(4)),
                  pl.BlockSpec((1, 2, HEAD_W), lambda b, h: (h, 0, 0)),
                  pl.BlockSpec((1, 1, HEAD_W), lambda b, h: (h, 0, 0)),
                  pl.BlockSpec(masks.shape, lambda b, h: (0, 0, 0))],
        out_specs=pl.BlockSpec((1, seq, HEAD_W), lambda b, h: (b, 0, h)),
        scratch_shapes=[pltpu.VMEM((seq, HEAD_W), F32),
                        pltpu.VMEM((seq, HEAD_W), F32)],
        compiler_params=_cparams(("parallel", "parallel")),
        name="hgrn2",
    )(hg3, hg3, hg3, hg3, hg3, lbs, gain.reshape(N_HEADS, 1, HEAD_W), masks)


def _merge_kernel(a_ref, b_ref, gate_ref, x_ref, wa_ref, wb_ref, wo_ref, g_ref, xo_ref, h_ref):
    ya = jnp.dot(a_ref[...], wa_ref[...], preferred_element_type=F32)
    yb = jnp.dot(b_ref[...], wb_ref[...], preferred_element_type=F32)
    merged = (gate_ref[:, :D_MODEL].astype(F32) * ya
              + gate_ref[:, D_MODEL:].astype(F32) * yb).astype(BF16)
    x_new = x_ref[...] + jnp.dot(merged, wo_ref[...], preferred_element_type=F32)
    xo_ref[...] = x_new
    h_ref[...] = _rms_scale(x_new, g_ref[...]).astype(h_ref.dtype)


def _merge(attn_o, hg_o, gates, x, wa, wb, wo, gain, tm):
    m = x.shape[0]
    row = lambda i: (i, 0)
    fixed = lambda i: (0, 0)
    return pl.pallas_call(
        _merge_kernel,
        out_shape=(jax.ShapeDtypeStruct((m, D_MODEL), F32),
                   jax.ShapeDtypeStruct((m, D_MODEL), BF16)),
        grid=(m // tm,),
        in_specs=[pl.BlockSpec((tm, MIX_W), row),
                  pl.BlockSpec((tm, MIX_W), row),
                  pl.BlockSpec((tm, GATE_W), row),
                  pl.BlockSpec((tm, D_MODEL), row),
                  pl.BlockSpec((MIX_W, D_MODEL), fixed),
                  pl.BlockSpec((MIX_W, D_MODEL), fixed),
                  pl.BlockSpec((D_MODEL, D_MODEL), fixed),
                  pl.BlockSpec((1, D_MODEL), fixed)],
        out_specs=(pl.BlockSpec((tm, D_MODEL), row),
                   pl.BlockSpec((tm, D_MODEL), row)),
        compiler_params=_cparams(("parallel",)),
        name="merge_out",
    )(attn_o, hg_o, gates, x, wa, wb, wo, gain.reshape(1, D_MODEL))


def _ffn_kernel(h_ref, x_ref, wg_ref, wu_ref, wd_ref, g_ref, *out_refs, emit_x):
    act_scr = out_refs[-1]
    h = h_ref[...]
    for c in range(FFN_HIDDEN // MXU_N):
        sl = slice(c * MXU_N, (c + 1) * MXU_N)
        gate = jnp.dot(h, wg_ref[:, sl], preferred_element_type=F32)
        up = jnp.dot(h, wu_ref[:, sl], preferred_element_type=F32)
        act_scr[:, sl] = (gate * _sigmoid(gate) * up).astype(BF16)
    x_new = x_ref[...] + jnp.dot(act_scr[...], wd_ref[...], preferred_element_type=F32)
    normed = _rms_scale(x_new, g_ref[...])
    if emit_x:
        out_refs[0][...] = x_new
        out_refs[1][...] = normed.astype(out_refs[1].dtype)
    else:
        out_refs[0][...] = normed.astype(out_refs[0].dtype)


def _ffn(h, x, wg, wu, wd, gain, tm, last):
    m = x.shape[0]
    row = lambda i: (i, 0)
    fixed = lambda i: (0, 0)
    if last:
        out_shape = jax.ShapeDtypeStruct((m, D_MODEL), F32)
        out_specs = pl.BlockSpec((tm, D_MODEL), row)
    else:
        out_shape = (jax.ShapeDtypeStruct((m, D_MODEL), F32),
                     jax.ShapeDtypeStruct((m, D_MODEL), BF16))
        out_specs = (pl.BlockSpec((tm, D_MODEL), row), pl.BlockSpec((tm, D_MODEL), row))
    return pl.pallas_call(
        functools.partial(_ffn_kernel, emit_x=not last),
        out_shape=out_shape,
        grid=(m // tm,),
        in_specs=[pl.BlockSpec((tm, D_MODEL), row),
                  pl.BlockSpec((tm, D_MODEL), row),
                  pl.BlockSpec((D_MODEL, FFN_HIDDEN), fixed),
                  pl.BlockSpec((D_MODEL, FFN_HIDDEN), fixed),
                  pl.BlockSpec((FFN_HIDDEN, D_MODEL), fixed),
                  pl.BlockSpec((1, D_MODEL), fixed)],
        out_specs=out_specs,
        scratch_shapes=[pltpu.VMEM((tm, FFN_HIDDEN), BF16)],
        compiler_params=_cparams(("parallel",)),
        name="ffn",
    )(h, x, wg, wu, wd, gain.reshape(1, D_MODEL))


def _rope_tables(positions):
    half = ROT_DIM // 2
    inv_freq = ROPE_THETA ** (-(jnp.arange(0, ROT_DIM, 2, dtype=F32) / ROT_DIM))
    ang = positions.astype(F32).reshape(-1, 1) * inv_freq
    cos, sin = jnp.cos(ang), jnp.sin(ang)
    m = ang.shape[0]
    pad = jnp.zeros((m, QK_DIM - ROT_DIM), F32)
    zeros = jnp.zeros((m, half), F32)
    cos_t = jnp.concatenate([cos, cos, pad + 1.0], axis=1)
    sin_a = jnp.concatenate([-sin, zeros, pad], axis=1)
    sin_b = jnp.concatenate([zeros, sin, pad], axis=1)
    tile2 = lambda t: jnp.concatenate([t, t], axis=1)
    return tile2(cos_t), tile2(sin_a), tile2(sin_b)


def kernel(x, positions, w_in, da_lambda, da_norm, hg_lb_logits, hg_norm, w_a, w_b, w_o,
           attn_norm, ffn_norm, w_gate, w_up, w_down, final_norm):
    batch, seq, _ = x.shape
    depth = w_in.shape[0]
    m = batch * seq
    tm = min(512, m)
    tq = min(512, seq)
    tk = min(512, seq)

    cos_t, sin_a, sin_b = _rope_tables(positions)
    p = jax.nn.softmax(hg_lb_logits.astype(F32), axis=1)
    csum = jnp.cumsum(p, axis=1)
    lbs = csum - csum[:, :1]

    xf = x.reshape(m, D_MODEL).astype(F32)
    h = _rmsnorm(xf, attn_norm[0], tm)
    out = None
    for layer in range(depth):
        lam_init = 0.8 - 0.6 * math.exp(-0.3 * layer)
        l32 = da_lambda[layer].astype(F32)
        lam_full = (jnp.exp(jnp.sum(l32[0] * l32[1])) - jnp.exp(jnp.sum(l32[2] * l32[3]))
                    + lam_init)
        lam = jnp.broadcast_to(lam_full, (1, HEAD_W)).astype(F32)
        lb_layer = lbs[:, layer].reshape(2, N_HEADS, HEAD_W).transpose(1, 0, 2)

        qkv, hg, gates = _in_proj(h, w_in[layer].astype(BF16), cos_t, sin_a, sin_b, tm)
        attn_o = _diff_attention(qkv, lam, da_norm[layer], batch, seq, tq, tk, 1.0 - lam_init)
        hg_o = _hgrn2(hg, lb_layer, hg_norm[layer], batch, seq, min(512, seq))
        xf, h2 = _merge(attn_o.reshape(m, MIX_W), hg_o.reshape(m, MIX_W), gates, xf,
                        w_a[layer].astype(BF16), w_b[layer].astype(BF16),
                        w_o[layer].astype(BF16), ffn_norm[layer], tm)
        last = layer == depth - 1
        next_gain = final_norm if last else attn_norm[layer + 1]
        res = _ffn(h2, xf, w_gate[layer].astype(BF16), w_up[layer].astype(BF16),
                   w_down[layer].astype(BF16), next_gain, tm, last)
        if last:
            out = res
        else:
            xf, h = res
    return out.reshape(batch, seq, D_MODEL).astype(x.dtype)
```

```python
import functools
import math

import numpy as np
import jax
import jax.numpy as jnp
from jax import lax
from jax.experimental import pallas as pl
from jax.experimental.pallas import tpu as pltpu

F32 = jnp.float32
BF16 = jnp.bfloat16

D_MODEL = 1024
N_HEADS = 4
HEAD_W = 128
QK_DIM = 64
ROT_DIM = 16
ROPE_THETA = 500000.0
FFN_HIDDEN = 2816
NORM_EPS = 1e-6
MIX_W = N_HEADS * HEAD_W
QKV_W = 3 * MIX_W
HG_W = 5 * MIX_W
GATE_W = 2 * D_MODEL
IN_WIDTH = QKV_W + HG_W + GATE_W

LANES = 128
MXU_N = 256
VMEM_LIMIT = 56 * 1024 * 1024

HG_CHUNK = 128
HG_LEVELS = HG_CHUNK.bit_length() - 1

NT_DIMS = (((1,), (1,)), ((), ()))
Q_SCALE = QK_DIM ** -0.5 * math.log2(math.e)
ONES_ROWS = 16


def _cparams(sem):
    return pltpu.CompilerParams(dimension_semantics=sem, vmem_limit_bytes=VMEM_LIMIT)


def _sigmoid(x):
    return 1.0 / (1.0 + jnp.exp(-x))


def _rms_scale(x, gain):
    ms = jnp.mean(x * x, axis=-1, keepdims=True)
    return x * lax.rsqrt(ms + NORM_EPS) * gain


def _rmsnorm_kernel(x_ref, g_ref, o_ref):
    o_ref[...] = _rms_scale(x_ref[...], g_ref[...]).astype(o_ref.dtype)


def _rmsnorm(x2d, gain, tm):
    m, d = x2d.shape
    return pl.pallas_call(
        _rmsnorm_kernel,
        out_shape=jax.ShapeDtypeStruct((m, d), BF16),
        grid=(m // tm,),
        in_specs=[pl.BlockSpec((tm, d), lambda i: (i, 0)),
                  pl.BlockSpec((1, d), lambda i: (0, 0))],
        out_specs=pl.BlockSpec((tm, d), lambda i: (i, 0)),
        compiler_params=_cparams(("parallel",)),
        name="rmsnorm0",
    )(x2d, gain.reshape(1, d))


def _proj_kernel(h_ref, w_ref, c_ref, sa_ref, sb_ref, qk_ref, vt_ref, hg_ref, gate_ref):
    h = h_ref[...]
    cos_t = c_ref[...]
    sin_a = sa_ref[...]
    sin_b = sb_ref[...]

    def rope(a, scale):
        outs = []
        for hh in range(N_HEADS):
            t = a[:, hh * HEAD_W:(hh + 1) * HEAD_W]
            r = (t * cos_t + pltpu.roll(t, HEAD_W - ROT_DIM // 2, 1) * sin_a
                 + pltpu.roll(t, ROT_DIM // 2, 1) * sin_b)
            outs.append(r * scale if scale != 1.0 else r)
        return jnp.concatenate(outs, axis=1)

    n_groups = IN_WIDTH // MIX_W
    for g in range(n_groups):
        acc = jnp.dot(h, w_ref[:, g * MIX_W:(g + 1) * MIX_W], preferred_element_type=F32)
        if g == 0:
            qk_ref[:, 0:MIX_W] = rope(acc, Q_SCALE).astype(BF16)
        elif g == 1:
            qk_ref[:, MIX_W:2 * MIX_W] = rope(acc, 1.0).astype(BF16)
        elif g == 2:
            vt_ref[0] = acc.T.astype(BF16)
        elif g < 8:
            hg_ref[:, (g - 3) * MIX_W:(g - 2) * MIX_W] = acc
        else:
            gate_ref[:, (g - 8) * MIX_W:(g - 7) * MIX_W] = _sigmoid(acc).astype(BF16)


def _in_proj(h, w, cos_t, sin_a, sin_b, tm):
    m = h.shape[0]
    row = lambda i: (i, 0)
    fixed = lambda i: (0, 0)
    return pl.pallas_call(
        _proj_kernel,
        out_shape=(jax.ShapeDtypeStruct((m, 2 * MIX_W), BF16),
                   jax.ShapeDtypeStruct((m // tm, MIX_W, tm), BF16),
                   jax.ShapeDtypeStruct((m, HG_W), F32),
                   jax.ShapeDtypeStruct((m, GATE_W), BF16)),
        grid=(m // tm,),
        in_specs=[pl.BlockSpec((tm, D_MODEL), row),
                  pl.BlockSpec((D_MODEL, IN_WIDTH), fixed),
                  pl.BlockSpec((tm, HEAD_W), row),
                  pl.BlockSpec((tm, HEAD_W), row),
                  pl.BlockSpec((tm, HEAD_W), row)],
        out_specs=(pl.BlockSpec((tm, 2 * MIX_W), row),
                   pl.BlockSpec((1, MIX_W, tm), lambda i: (i, 0, 0)),
                   pl.BlockSpec((tm, HG_W), row),
                   pl.BlockSpec((tm, GATE_W), row)),
        compiler_params=_cparams(("parallel",)),
        name="in_proj",
    )(h, w, cos_t, sin_a, sin_b)


def _attn_kernel(lam_ref, q_ref, k_ref, vt_ref, g_ref, o_ref, s_scr, acc_scr,
                 *, tq, tk, out_scale):
    nk = k_ref.shape[1] // tk
    q = q_ref[0]
    lane = lax.broadcasted_iota(jnp.int32, (tq, HEAD_W), 1)
    zero = jnp.zeros_like(q)
    qs = jnp.concatenate([jnp.where(lane < QK_DIM, q, zero),
                          jnp.where(lane >= QK_DIM, q, zero)], axis=0)
    ones = jnp.ones((ONES_ROWS, tk), BF16)
    acc_scr[...] = jnp.zeros(acc_scr.shape, F32)

    def logits(j, slot):
        off = pl.multiple_of(j * tk, tk)
        st = lax.dot_general(k_ref[0, pl.ds(off, tk), :], qs, NT_DIMS,
                             preferred_element_type=F32)
        s_scr[slot] = st
        return jnp.max(st, axis=0, keepdims=True)

    def accumulate(j, slot, m_old, mx):
        m_new = jnp.maximum(m_old, mx)
        alpha = jnp.exp2(m_old - m_new)
        p = jnp.exp2(s_scr[slot] - m_new).astype(BF16)
        vt_ext = jnp.concatenate([vt_ref[j], ones], axis=0)
        acc_scr[...] = acc_scr[...] * alpha + jnp.dot(vt_ext, p, preferred_element_type=F32)
        return m_new

    def pair(jj, carry, last):
        m, mx = carry
        j = 2 * jj
        mx1 = logits(j + 1, 1)
        m = accumulate(j, 0, m, mx)
        mx2 = mx1 if last else logits(j + 2, 0)
        m = accumulate(j + 1, 1, m, mx1)
        return m, mx2

    m0 = jnp.full((1, 2 * tq), -jnp.inf, F32)
    carry = lax.fori_loop(0, nk // 2 - 1, functools.partial(pair, last=False), (m0, logits(0, 0)))
    pair(nk // 2 - 1, carry, True)

    acc = acc_scr[...]
    o_t = acc[:HEAD_W, :] / acc[HEAD_W:HEAD_W + 1, :]
    d_t = o_t[:, :tq] - lam_ref[0] * o_t[:, tq:]
    o_ref[0] = (_rms_scale(d_t.T, g_ref[0]) * out_scale).astype(o_ref.dtype)


def _diff_attention(qk, vt, lam, gain, batch, seq, tq, tk, out_scale):
    qk3 = qk.reshape(batch, seq, 2 * MIX_W)
    nk = seq // tk
    assert nk % 2 == 0, "the attention pipeline walks key chunks in pairs"
    kern = functools.partial(_attn_kernel, tq=tq, tk=tk, out_scale=out_scale)
    return pl.pallas_call(
        kern,
        out_shape=jax.ShapeDtypeStruct((batch, seq, MIX_W), BF16),
        grid=(batch, N_HEADS, seq // tq),
        in_specs=[pl.BlockSpec(memory_space=pltpu.SMEM),
                  pl.BlockSpec((1, tq, HEAD_W), lambda b, h, i: (b, i, h)),
                  pl.BlockSpec((1, seq, HEAD_W), lambda b, h, i: (b, 0, N_HEADS + h)),
                  pl.BlockSpec((nk, HEAD_W, tk), lambda b, h, i: (b, h, 0)),
                  pl.BlockSpec((1, 1, HEAD_W), lambda b, h, i: (h, 0, 0))],
        out_specs=pl.BlockSpec((1, tq, HEAD_W), lambda b, h, i: (b, i, h)),
        scratch_shapes=[pltpu.VMEM((2, tk, 2 * tq), F32),
                        pltpu.VMEM((HEAD_W + ONES_ROWS, 2 * tq), F32)],
        compiler_params=_cparams(("parallel", "parallel", "arbitrary")),
        name="diff_attn",
    )(lam, qk3, qk3, vt, gain.reshape(N_HEADS, 1, HEAD_W))


def _hg_masks():
    c = HG_CHUNK
    t = np.arange(c)[:, None]
    s = np.arange(c)[None, :]
    fwd = []
    for lvl in range(HG_LEVELS):
        h = 1 << lvl
        same = (t // (2 * h)) == (s // (2 * h))
        fwd.append(same & ((t & h) != 0) & ((s & h) == 0))
    bwd = [m.T for m in fwd]
    eye = t == s
    return np.stack(fwd + bwd + [eye]).astype(np.float32)


def _hg_chunk(q, z, v, st_t, lb, mask_ref, rev):
    c = HG_CHUNK
    oml = 1.0 - lb
    ez = jnp.exp(-jnp.abs(z))
    r = 1.0 / (1.0 + ez)
    er = ez * r
    pos = z >= 0.0
    f = lb + oml * jnp.where(pos, r, er)
    k = oml * jnp.where(pos, er, r)
    row = lax.broadcasted_iota(jnp.int32, (c, HEAD_W), 0)
    a_in = f
    e_ex = jnp.ones_like(f)
    tot = f
    sc = jnp.sum(q * k, axis=1, keepdims=True) * mask_ref[2 * HG_LEVELS]
    for lvl in range(HG_LEVELS):
        h = 1 << lvl
        bit = (row & h) != 0
        qside = jnp.logical_not(bit) if rev else bit
        x = jnp.where(qside, q * a_in, k * e_ex).astype(BF16)
        lv = lax.dot_general(x, x, NT_DIMS, preferred_element_type=F32)
        sc = sc + lv * mask_ref[(HG_LEVELS if rev else 0) + lvl]
        prev = pltpu.roll(tot, h, 0)
        nxt = pltpu.roll(tot, c - h, 0)
        if rev:
            a_in = a_in * jnp.where(bit, 1.0, nxt)
            e_ex = e_ex * jnp.where(bit, prev, 1.0)
        else:
            a_in = a_in * jnp.where(bit, prev, 1.0)
            e_ex = e_ex * jnp.where(bit, 1.0, nxt)
        tot = tot * jnp.where(bit, prev, nxt)
    vb = v.astype(BF16)
    intra = jnp.dot(sc.astype(BF16), vb, preferred_element_type=F32)
    inter = lax.dot_general((q * a_in).astype(BF16), st_t.astype(BF16), NT_DIMS,
                            preferred_element_type=F32)
    upd = jnp.dot(v.T.astype(BF16), (k * e_ex).astype(BF16), preferred_element_type=F32)
    st_t = st_t * tot[0:1, :] + upd
    return inter + intra, st_t


def _hgrn_kernel(q_ref, zf_ref, zb_ref, v_ref, g_ref, lb_ref, gain_ref, mask_ref, o_ref,
                 of_scr, ob_scr, *, tr):
    seq = q_ref.shape[1]
    c = HG_CHUNK
    n = seq // c
    lb_f = lb_ref[0, 0:1, :]
    lb_b = lb_ref[0, 1:2, :]

    def body(i, carry):
        st_f, st_b = carry
        cf = pl.multiple_of(i * c, c)
        cb = pl.multiple_of((n - 1 - i) * c, c)
        o_f, st_f = _hg_chunk(q_ref[0, pl.ds(cf, c), :], zf_ref[0, pl.ds(cf, c), :],
                              v_ref[0, pl.ds(cf, c), :], st_f, lb_f, mask_ref, False)
        of_scr[pl.ds(cf, c), :] = o_f
        o_b, st_b = _hg_chunk(q_ref[0, pl.ds(cb, c), :], zb_ref[0, pl.ds(cb, c), :],
                              v_ref[0, pl.ds(cb, c), :], st_b, lb_b, mask_ref, True)
        ob_scr[pl.ds(cb, c), :] = o_b
        return st_f, st_b

    zero = jnp.zeros((HEAD_W, HEAD_W), F32)
    lax.fori_loop(0, n, body, (zero, zero), unroll=2 if n % 2 == 0 else 1)

    gain = gain_ref[0]

    def finish(i, carry):
        off = pl.multiple_of(i * tr, tr)
        o = of_scr[pl.ds(off, tr), :] + ob_scr[pl.ds(off, tr), :]
        y = _rms_scale(o, gain) * _sigmoid(g_ref[0, pl.ds(off, tr), :])
        o_ref[0, pl.ds(off, tr), :] = y.astype(o_ref.dtype)
        return carry

    lax.fori_loop(0, seq // tr, finish, 0)


def _hgrn2(hg, lbs, gain, batch, seq, tr):
    hg3 = hg.reshape(batch, seq, HG_W)
    masks = jnp.asarray(_hg_masks())
    col = lambda g: (lambda b, h: (b, 0, g * N_HEADS + h))
    kern = functools.partial(_hgrn_kernel, tr=tr)
    return pl.pallas_call(
        kern,
        out_shape=jax.ShapeDtypeStruct((batch, seq, MIX_W), BF16),
        grid=(batch, N_HEADS),
        in_specs=[pl.BlockSpec((1, seq, HEAD_W), col(0)),
                  pl.BlockSpec((1, seq, HEAD_W), col(1)),
                  pl.BlockSpec((1, seq, HEAD_W), col(2)),
                  pl.BlockSpec((1, seq, HEAD_W), col(3)),
                  pl.BlockSpec((1, seq, HEAD_W), col(4)),
                  pl.BlockSpec((1, 2, HEAD_W), lambda b, h: (h, 0, 0)),
                  pl.BlockSpec((1, 1, HEAD_W), lambda b, h: (h, 0, 0)),
                  pl.BlockSpec(masks.shape, lambda b, h: (0, 0, 0))],
        out_specs=pl.BlockSpec((1, seq, HEAD_W), lambda b, h: (b, 0, h)),
        scratch_shapes=[pltpu.VMEM((seq, HEAD_W), F32),
                        pltpu.VMEM((seq, HEAD_W), F32)],
        compiler_params=_cparams(("parallel", "parallel")),
        name="hgrn2",
    )(hg3, hg3, hg3, hg3, hg3, lbs, gain.reshape(N_HEADS, 1, HEAD_W), masks)


def _merge_kernel(a_ref, b_ref, gate_ref, x_ref, wa_ref, wb_ref, wo_ref, g_ref, xo_ref, h_ref):
    ya = jnp.dot(a_ref[...], wa_ref[...], preferred_element_type=F32)
    yb = jnp.dot(b_ref[...], wb_ref[...], preferred_element_type=F32)
    merged = (gate_ref[:, :D_MODEL].astype(F32) * ya
              + gate_ref[:, D_MODEL:].astype(F32) * yb).astype(BF16)
    x_new = x_ref[...] + jnp.dot(merged, wo_ref[...], preferred_element_type=F32)
    xo_ref[...] = x_new
    h_ref[...] = _rms_scale(x_new, g_ref[...]).astype(h_ref.dtype)


def _merge(attn_o, hg_o, gates, x, wa, wb, wo, gain, tm):
    m = x.shape[0]
    row = lambda i: (i, 0)
    fixed = lambda i: (0, 0)
    return pl.pallas_call(
        _merge_kernel,
        out_shape=(jax.ShapeDtypeStruct((m, D_MODEL), F32),
                   jax.ShapeDtypeStruct((m, D_MODEL), BF16)),
        grid=(m // tm,),
        in_specs=[pl.BlockSpec((tm, MIX_W), row),
                  pl.BlockSpec((tm, MIX_W), row),
                  pl.BlockSpec((tm, GATE_W), row),
                  pl.BlockSpec((tm, D_MODEL), row),
                  pl.BlockSpec((MIX_W, D_MODEL), fixed),
                  pl.BlockSpec((MIX_W, D_MODEL), fixed),
                  pl.BlockSpec((D_MODEL, D_MODEL), fixed),
                  pl.BlockSpec((1, D_MODEL), fixed)],
        out_specs=(pl.BlockSpec((tm, D_MODEL), row),
                   pl.BlockSpec((tm, D_MODEL), row)),
        compiler_params=_cparams(("parallel",)),
        name="merge_out",
    )(attn_o, hg_o, gates, x, wa, wb, wo, gain.reshape(1, D_MODEL))


def _ffn_kernel(h_ref, x_ref, wg_ref, wu_ref, wd_ref, g_ref, *out_refs, emit_x):
    act_scr = out_refs[-1]
    h = h_ref[...]
    for c in range(FFN_HIDDEN // MXU_N):
        sl = slice(c * MXU_N, (c + 1) * MXU_N)
        gate = jnp.dot(h, wg_ref[:, sl], preferred_element_type=F32)
        up = jnp.dot(h, wu_ref[:, sl], preferred_element_type=F32)
        act_scr[:, sl] = (gate * _sigmoid(gate) * up).astype(BF16)
    x_new = x_ref[...] + jnp.dot(act_scr[...], wd_ref[...], preferred_element_type=F32)
    normed = _rms_scale(x_new, g_ref[...])
    if emit_x:
        out_refs[0][...] = x_new
        out_refs[1][...] = normed.astype(out_refs[1].dtype)
    else:
        out_refs[0][...] = normed.astype(out_refs[0].dtype)


def _ffn(h, x, wg, wu, wd, gain, tm, last):
    m = x.shape[0]
    row = lambda i: (i, 0)
    fixed = lambda i: (0, 0)
    if last:
        out_shape = jax.ShapeDtypeStruct((m, D_MODEL), F32)
        out_specs = pl.BlockSpec((tm, D_MODEL), row)
    else:
        out_shape = (jax.ShapeDtypeStruct((m, D_MODEL), F32),
                     jax.ShapeDtypeStruct((m, D_MODEL), BF16))
        out_specs = (pl.BlockSpec((tm, D_MODEL), row), pl.BlockSpec((tm, D_MODEL), row))
    return pl.pallas_call(
        functools.partial(_ffn_kernel, emit_x=not last),
        out_shape=out_shape,
        grid=(m // tm,),
        in_specs=[pl.BlockSpec((tm, D_MODEL), row),
                  pl.BlockSpec((tm, D_MODEL), row),
                  pl.BlockSpec((D_MODEL, FFN_HIDDEN), fixed),
                  pl.BlockSpec((D_MODEL, FFN_HIDDEN), fixed),
                  pl.BlockSpec((FFN_HIDDEN, D_MODEL), fixed),
                  pl.BlockSpec((1, D_MODEL), fixed)],
        out_specs=out_specs,
        scratch_shapes=[pltpu.VMEM((tm, FFN_HIDDEN), BF16)],
        compiler_params=_cparams(("parallel",)),
        name="ffn",
    )(h, x, wg, wu, wd, gain.reshape(1, D_MODEL))


def _rope_tables(positions):
    half = ROT_DIM // 2
    inv_freq = ROPE_THETA ** (-(jnp.arange(0, ROT_DIM, 2, dtype=F32) / ROT_DIM))
    ang = positions.astype(F32).reshape(-1, 1) * inv_freq
    cos, sin = jnp.cos(ang), jnp.sin(ang)
    m = ang.shape[0]
    pad = jnp.zeros((m, QK_DIM - ROT_DIM), F32)
    zeros = jnp.zeros((m, half), F32)
    cos_t = jnp.concatenate([cos, cos, pad + 1.0], axis=1)
    sin_a = jnp.concatenate([-sin, zeros, pad], axis=1)
    sin_b = jnp.concatenate([zeros, sin, pad], axis=1)
    tile2 = lambda t: jnp.concatenate([t, t], axis=1)
    return tile2(cos_t), tile2(sin_a), tile2(sin_b)


def kernel(x, positions, w_in, da_lambda, da_norm, hg_lb_logits, hg_norm, w_a, w_b, w_o,
           attn_norm, ffn_norm, w_gate, w_up, w_down, final_norm):
    batch, seq, _ = x.shape
    depth = w_in.shape[0]
    m = batch * seq
    tm = min(512, seq)
    tq = tk = tm

    cos_t, sin_a, sin_b = _rope_tables(positions)
    p = jax.nn.softmax(hg_lb_logits.astype(F32), axis=1)
    csum = jnp.cumsum(p, axis=1)
    lbs = csum - csum[:, :1]

    xf = x.reshape(m, D_MODEL).astype(F32)
    h = _rmsnorm(xf, attn_norm[0], tm)
    out = None
    for layer in range(depth):
        lam_init = 0.8 - 0.6 * math.exp(-0.3 * layer)
        l32 = da_lambda[layer].astype(F32)
        lam_full = (jnp.exp(jnp.sum(l32[0] * l32[1])) - jnp.exp(jnp.sum(l32[2] * l32[3]))
                    + lam_init)
        lam = lam_full.reshape(1).astype(F32)
        lb_layer = lbs[:, layer].reshape(2, N_HEADS, HEAD_W).transpose(1, 0, 2)

        qk, vt, hg, gates = _in_proj(h, w_in[layer].astype(BF16), cos_t, sin_a, sin_b, tm)
        attn_o = _diff_attention(qk, vt, lam, da_norm[layer], batch, seq, tq, tk, 1.0 - lam_init)
        hg_o = _hgrn2(hg, lb_layer, hg_norm[layer], batch, seq, min(512, seq))
        xf, h2 = _merge(attn_o.reshape(m, MIX_W), hg_o.reshape(m, MIX_W), gates, xf,
                        w_a[layer].astype(BF16), w_b[layer].astype(BF16),
                        w_o[layer].astype(BF16), ffn_norm[layer], tm)
        last = layer == depth - 1
        next_gain = final_norm if last else attn_norm[layer + 1]
        res = _ffn(h2, xf, w_gate[layer].astype(BF16), w_up[layer].astype(BF16),
                   w_down[layer].astype(BF16), next_gain, tm, last)
        if last:
            out = res
        else:
            xf, h = res
    return out.reshape(batch, seq, D_MODEL).astype(x.dtype)
```

```python
import functools
import math

import numpy as np
import jax
import jax.numpy as jnp
from jax import lax
from jax.experimental import pallas as pl
from jax.experimental.pallas import tpu as pltpu

F32 = jnp.float32
BF16 = jnp.bfloat16

D_MODEL = 1024
N_HEADS = 4
HEAD_W = 128
QK_DIM = 64
ROT_DIM = 16
ROPE_THETA = 500000.0
FFN_HIDDEN = 2816
NORM_EPS = 1e-6
MIX_W = N_HEADS * HEAD_W
HG_W = 5 * MIX_W
GATE_W = 2 * D_MODEL
IN_WIDTH = 3 * MIX_W + HG_W + GATE_W

LANES = 128
SUBLANES = 8
MXU_N = 256
VMEM_LIMIT = 56 * 1024 * 1024

HG_CHUNK = 128
HG_LEVELS = HG_CHUNK.bit_length() - 1
HG_LOW_LEVELS = SUBLANES.bit_length() - 1
HG_TILES = HG_CHUNK // SUBLANES

NT_DIMS = (((1,), (1,)), ((), ()))
Q_SCALE = QK_DIM ** -0.5 * math.log2(math.e)
ONES_ROWS = 16


def _cparams(sem):
    return pltpu.CompilerParams(dimension_semantics=sem, vmem_limit_bytes=VMEM_LIMIT)


def _sigmoid(x):
    return 1.0 / (1.0 + jnp.exp(-x))


def _rms_scale(x, gain):
    ms = jnp.mean(x * x, axis=-1, keepdims=True)
    return x * lax.rsqrt(ms + NORM_EPS) * gain


def _layer_spec(shape, layer):
    zeros = (0,) * len(shape)
    return pl.BlockSpec((None,) + tuple(shape), lambda *_: (layer,) + zeros)


def _rmsnorm_kernel(x_ref, g_ref, o_ref):
    o_ref[...] = _rms_scale(x_ref[...], g_ref[...]).astype(o_ref.dtype)


def _rmsnorm(x2d, gains, tm):
    m, d = x2d.shape
    return pl.pallas_call(
        _rmsnorm_kernel,
        out_shape=jax.ShapeDtypeStruct((m, d), BF16),
        grid=(m // tm,),
        in_specs=[pl.BlockSpec((tm, d), lambda i: (i, 0)),
                  _layer_spec((1, d), 0)],
        out_specs=pl.BlockSpec((tm, d), lambda i: (i, 0)),
        compiler_params=_cparams(("parallel",)),
        name="rmsnorm0",
    )(x2d, gains)


def _proj_kernel(h_ref, w_ref, rope_ref, qk_ref, vt_ref, hg_ref, gate_ref):
    h = h_ref[...]
    cos_t = rope_ref[:, 0:HEAD_W]
    sin_a = rope_ref[:, HEAD_W:2 * HEAD_W]
    sin_b = rope_ref[:, 2 * HEAD_W:3 * HEAD_W]

    def rope(a, scale):
        outs = []
        for hh in range(N_HEADS):
            t = a[:, hh * HEAD_W:(hh + 1) * HEAD_W]
            r = (t * cos_t + pltpu.roll(t, HEAD_W - ROT_DIM // 2, 1) * sin_a
                 + pltpu.roll(t, ROT_DIM // 2, 1) * sin_b)
            outs.append(r * scale if scale != 1.0 else r)
        return jnp.concatenate(outs, axis=1)

    n_groups = IN_WIDTH // MIX_W
    for g in range(n_groups):
        acc = jnp.dot(h, w_ref[:, g * MIX_W:(g + 1) * MIX_W], preferred_element_type=F32)
        if g == 0:
            qk_ref[:, 0:MIX_W] = rope(acc, Q_SCALE).astype(BF16)
        elif g == 1:
            qk_ref[:, MIX_W:2 * MIX_W] = rope(acc, 1.0).astype(BF16)
        elif g == 2:
            vt_ref[0] = acc.T.astype(BF16)
        elif g < 8:
            hg_ref[:, (g - 3) * MIX_W:(g - 2) * MIX_W] = acc
        else:
            gate_ref[:, (g - 8) * MIX_W:(g - 7) * MIX_W] = _sigmoid(acc).astype(BF16)


def _in_proj(h, w_all, layer, rope_t, tm):
    m = h.shape[0]
    row = lambda i: (i, 0)
    return pl.pallas_call(
        _proj_kernel,
        out_shape=(jax.ShapeDtypeStruct((m, 2 * MIX_W), BF16),
                   jax.ShapeDtypeStruct((m // tm, MIX_W, tm), BF16),
                   jax.ShapeDtypeStruct((m, HG_W), F32),
                   jax.ShapeDtypeStruct((m, GATE_W), BF16)),
        grid=(m // tm,),
        in_specs=[pl.BlockSpec((tm, D_MODEL), row),
                  _layer_spec((D_MODEL, IN_WIDTH), layer),
                  pl.BlockSpec((tm, 3 * HEAD_W), row)],
        out_specs=(pl.BlockSpec((tm, 2 * MIX_W), row),
                   pl.BlockSpec((1, MIX_W, tm), lambda i: (i, 0, 0)),
                   pl.BlockSpec((tm, HG_W), row),
                   pl.BlockSpec((tm, GATE_W), row)),
        compiler_params=_cparams(("parallel",)),
        name="in_proj",
    )(h, w_all, rope_t)


def _attn_kernel(lam_ref, q_ref, k_ref, vt_ref, g_ref, o_ref, s_scr, acc_scr,
                 *, tq, tk, out_scale):
    seq = k_ref.shape[1]
    nk = seq // tk
    nq = seq // tq
    lane = lax.broadcasted_iota(jnp.int32, (tq, HEAD_W), 1)
    ones = jnp.ones((ONES_ROWS, tk), BF16)
    m0 = jnp.full((1, 2 * tq), -jnp.inf, F32)
    lam = lam_ref[0]
    gain = g_ref[0]

    def logits(u, j, slot):
        q = q_ref[0, pl.ds(pl.multiple_of(u * tq, tq), tq), :]
        zero = jnp.zeros_like(q)
        qs = jnp.concatenate([jnp.where(lane < QK_DIM, q, zero),
                              jnp.where(lane >= QK_DIM, q, zero)], axis=0)
        st = lax.dot_general(k_ref[0, pl.ds(pl.multiple_of(j * tk, tk), tk), :], qs, NT_DIMS,
                             preferred_element_type=F32)
        s_scr[slot] = st
        return jnp.max(st, axis=0, keepdims=True)

    def accumulate(j, slot, m_old, mx):
        m_new = jnp.maximum(m_old, mx)
        alpha = jnp.exp2(m_old - m_new)
        p = jnp.exp2(s_scr[slot] - m_new).astype(BF16)
        vt_ext = jnp.concatenate([vt_ref[j], ones], axis=0)
        acc_scr[...] = acc_scr[...] * alpha + jnp.dot(vt_ext, p, preferred_element_type=F32)
        return m_new

    def pair(jj, carry, u):
        m, mx = carry
        j = 2 * jj
        mx1 = logits(u, j + 1, 1)
        m = accumulate(j, 0, m, mx)
        mx2 = logits(u, j + 2, 0)
        m = accumulate(j + 1, 1, m, mx1)
        return m, mx2

    def tile(u, mx):
        m, mx = lax.fori_loop(0, nk // 2 - 1, functools.partial(pair, u=u), (m0, mx))
        mx1 = logits(u, nk - 1, 1)
        m = accumulate(nk - 2, 0, m, mx)
        mx_next = logits(jnp.minimum(u + 1, nq - 1), 0, 0)
        accumulate(nk - 1, 1, m, mx1)
        acc = acc_scr[...]
        o_t = acc[:HEAD_W, :] / acc[HEAD_W:HEAD_W + 1, :]
        d_t = o_t[:, :tq] - lam * o_t[:, tq:]
        o_ref[0, pl.ds(pl.multiple_of(u * tq, tq), tq), :] = (
            _rms_scale(d_t.T, gain) * out_scale).astype(o_ref.dtype)
        return mx_next

    acc_scr[...] = jnp.zeros(acc_scr.shape, F32)
    lax.fori_loop(0, nq, tile, logits(0, 0, 0))


def _diff_attention(qk, vt, lam, gains, layer, batch, seq, tq, tk, out_scale):
    qk3 = qk.reshape(batch, seq, 2 * MIX_W)
    nk = seq // tk
    assert nk % 2 == 0, "the attention pipeline walks key chunks in pairs"
    kern = functools.partial(_attn_kernel, tq=tq, tk=tk, out_scale=out_scale)
    return pl.pallas_call(
        kern,
        out_shape=jax.ShapeDtypeStruct((batch, seq, MIX_W), BF16),
        grid=(batch, N_HEADS),
        in_specs=[pl.BlockSpec(memory_space=pltpu.SMEM),
                  pl.BlockSpec((1, seq, HEAD_W), lambda b, h: (b, 0, h)),
                  pl.BlockSpec((1, seq, HEAD_W), lambda b, h: (b, 0, N_HEADS + h)),
                  pl.BlockSpec((nk, HEAD_W, tk), lambda b, h: (b, h, 0)),
                  pl.BlockSpec((None, 1, 1, HEAD_W), lambda b, h: (layer, h, 0, 0))],
        out_specs=pl.BlockSpec((1, seq, HEAD_W), lambda b, h: (b, 0, h)),
        scratch_shapes=[pltpu.VMEM((2, tk, 2 * tq), F32),
                        pltpu.VMEM((HEAD_W + ONES_ROWS, 2 * tq), F32)],
        compiler_params=_cparams(("parallel", "parallel")),
        name="diff_attn",
    )(lam, qk3, qk3, vt, gains)


def _hg_masks():
    c = HG_CHUNK
    t = np.arange(c)[:, None]
    s = np.arange(c)[None, :]
    full = []
    for lvl in range(HG_LEVELS):
        h = 1 << lvl
        same = (t // (2 * h)) == (s // (2 * h))
        full.append(same & ((t & h) != 0) & ((s & h) == 0))
    low = ([full[l] for l in range(HG_LOW_LEVELS)] + [full[l].T for l in range(HG_LOW_LEVELS)]
           + [t == s])
    high = []
    for rev in (False, True):
        for lvl in range(HG_LOW_LEVELS, HG_LEVELS):
            h = 1 << lvl
            rows = np.arange(c)
            qrows = rows[(rows & h) == 0] if rev else rows[(rows & h) != 0]
            mk = full[lvl].T if rev else full[lvl]
            high.append(mk[qrows])
    return np.stack(low).astype(np.float32), np.stack(high).astype(np.float32)


def _hg_chunk(q, z, v, st_t, lb, mlow_ref, mhigh_ref, rev):
    c = HG_CHUNK
    oml = 1.0 - lb
    ez = jnp.exp(-jnp.abs(z))
    r = 1.0 / (1.0 + ez)
    er = ez * r
    pos = z >= 0.0
    f = lb + oml * jnp.where(pos, r, er)
    k = oml * jnp.where(pos, er, r)
    row = lax.broadcasted_iota(jnp.int32, (c, HEAD_W), 0)
    a_in = f
    e_ex = jnp.ones_like(f)
    tot = f
    sc = jnp.sum(q * k, axis=1, keepdims=True) * mlow_ref[2 * HG_LOW_LEVELS]
    for lvl in range(HG_LOW_LEVELS):
        h = 1 << lvl
        bit = (row & h) != 0
        qside = jnp.logical_not(bit) if rev else bit
        x = jnp.where(qside, q * a_in, k * e_ex).astype(BF16)
        lv = lax.dot_general(x, x, NT_DIMS, preferred_element_type=F32)
        sc = sc + lv * mlow_ref[(HG_LOW_LEVELS if rev else 0) + lvl]
        tot3 = tot.reshape(HG_TILES, SUBLANES, HEAD_W)
        prev = pltpu.roll(tot3, h, 1).reshape(c, HEAD_W)
        nxt = pltpu.roll(tot3, SUBLANES - h, 1).reshape(c, HEAD_W)
        if rev:
            a_in = a_in * jnp.where(bit, 1.0, nxt)
            e_ex = e_ex * jnp.where(bit, prev, 1.0)
        else:
            a_in = a_in * jnp.where(bit, prev, 1.0)
            e_ex = e_ex * jnp.where(bit, 1.0, nxt)
        tot = tot * jnp.where(bit, prev, nxt)

    tiles = lambda arr: [arr[i * SUBLANES:(i + 1) * SUBLANES] for i in range(HG_TILES)]
    q_t, k_t, a_t, e_t, sc_t = tiles(q), tiles(k), tiles(a_in), tiles(e_ex), tiles(sc)
    tb = tiles(tot)
    for lvl in range(HG_LOW_LEVELS, HG_LEVELS):
        ht = (1 << lvl) // SUBLANES
        n_blocks = HG_TILES // (2 * ht)
        x_t = [None] * HG_TILES
        q_idx = []
        for b in range(n_blocks):
            lo = range(2 * b * ht, (2 * b + 1) * ht)
            hi = range((2 * b + 1) * ht, (2 * b + 2) * ht)
            q_half, k_half = (lo, hi) if rev else (hi, lo)
            for i in q_half:
                x_t[i] = q_t[i] * a_t[i]
            for i in k_half:
                x_t[i] = k_t[i] * e_t[i]
            q_idx.extend(q_half)
        x = jnp.concatenate(x_t, axis=0).astype(BF16)
        xq = jnp.concatenate([x_t[i] for i in q_idx], axis=0).astype(BF16)
        lv = lax.dot_general(xq, x, NT_DIMS, preferred_element_type=F32)
        mk = mhigh_ref[(HG_LEVELS - HG_LOW_LEVELS if rev else 0) + lvl - HG_LOW_LEVELS]
        for n, i in enumerate(q_idx):
            rows = slice(n * SUBLANES, (n + 1) * SUBLANES)
            sc_t[i] = sc_t[i] + lv[rows] * mk[rows]
        new_tb = []
        for b in range(n_blocks):
            t_lo, t_hi = tb[2 * b], tb[2 * b + 1]
            lo = range(2 * b * ht, (2 * b + 1) * ht)
            hi = range((2 * b + 1) * ht, (2 * b + 2) * ht)
            if rev:
                for i in lo:
                    a_t[i] = a_t[i] * t_hi
                for i in hi:
                    e_t[i] = e_t[i] * t_lo
            else:
                for i in hi:
                    a_t[i] = a_t[i] * t_lo
                for i in lo:
                    e_t[i] = e_t[i] * t_hi
            new_tb.append(t_lo * t_hi)
        tb = new_tb

    sc = jnp.concatenate(sc_t, axis=0)
    qa = jnp.concatenate([q_t[i] * a_t[i] for i in range(HG_TILES)], axis=0)
    ke = jnp.concatenate([k_t[i] * e_t[i] for i in range(HG_TILES)], axis=0)
    vb = v.astype(BF16)
    intra = jnp.dot(sc.astype(BF16), vb, preferred_element_type=F32)
    inter = lax.dot_general(qa.astype(BF16), st_t.astype(BF16), NT_DIMS,
                            preferred_element_type=F32)
    upd = jnp.dot(v.T.astype(BF16), ke.astype(BF16), preferred_element_type=F32)
    st_t = st_t * tb[0][0:1, :] + upd
    return inter + intra, st_t


def _hgrn_kernel(q_ref, zf_ref, zb_ref, v_ref, g_ref, lb_ref, gain_ref, mlow_ref, mhigh_ref,
                 o_ref, of_scr, ob_scr, *, tr):
    seq = q_ref.shape[1]
    c = HG_CHUNK
    n = seq // c
    lb_f = lb_ref[0, 0:1, :]
    lb_b = lb_ref[0, 1:2, :]

    def body(i, carry):
        st_f, st_b = carry
        cf = pl.multiple_of(i * c, c)
        cb = pl.multiple_of((n - 1 - i) * c, c)
        o_f, st_f = _hg_chunk(q_ref[0, pl.ds(cf, c), :], zf_ref[0, pl.ds(cf, c), :],
                              v_ref[0, pl.ds(cf, c), :], st_f, lb_f, mlow_ref, mhigh_ref, False)
        of_scr[pl.ds(cf, c), :] = o_f
        o_b, st_b = _hg_chunk(q_ref[0, pl.ds(cb, c), :], zb_ref[0, pl.ds(cb, c), :],
                              v_ref[0, pl.ds(cb, c), :], st_b, lb_b, mlow_ref, mhigh_ref, True)
        ob_scr[pl.ds(cb, c), :] = o_b
        return st_f, st_b

    zero = jnp.zeros((HEAD_W, HEAD_W), F32)
    lax.fori_loop(0, n, body, (zero, zero), unroll=2 if n % 2 == 0 else 1)

    gain = gain_ref[0]

    def finish(i, carry):
        off = pl.multiple_of(i * tr, tr)
        o = of_scr[pl.ds(off, tr), :] + ob_scr[pl.ds(off, tr), :]
        y = _rms_scale(o, gain) * _sigmoid(g_ref[0, pl.ds(off, tr), :])
        o_ref[0, pl.ds(off, tr), :] = y.astype(o_ref.dtype)
        return carry

    lax.fori_loop(0, seq // tr, finish, 0)


def _hgrn2(hg, lbs, gains, layer, batch, seq, tr):
    hg3 = hg.reshape(batch, seq, HG_W)
    mlow, mhigh = (jnp.asarray(a) for a in _hg_masks())
    col = lambda g: (lambda b, h: (b, 0, g * N_HEADS + h))
    kern = functools.partial(_hgrn_kernel, tr=tr)
    return pl.pallas_call(
        kern,
        out_shape=jax.ShapeDtypeStruct((batch, seq, MIX_W), BF16),
        grid=(batch, N_HEADS),
        in_specs=[pl.BlockSpec((1, seq, HEAD_W), col(0)),
                  pl.BlockSpec((1, seq, HEAD_W), col(1)),
                  pl.BlockSpec((1, seq, HEAD_W), col(2)),
                  pl.BlockSpec((1, seq, HEAD_W), col(3)),
                  pl.BlockSpec((1, seq, HEAD_W), col(4)),
                  pl.BlockSpec((1, 2, HEAD_W), lambda b, h: (h, 0, 0)),
                  pl.BlockSpec((None, 1, 1, HEAD_W), lambda b, h: (layer, h, 0, 0)),
                  pl.BlockSpec(mlow.shape, lambda b, h: (0, 0, 0)),
                  pl.BlockSpec(mhigh.shape, lambda b, h: (0, 0, 0))],
        out_specs=pl.BlockSpec((1, seq, HEAD_W), lambda b, h: (b, 0, h)),
        scratch_shapes=[pltpu.VMEM((seq, HEAD_W), F32),
                        pltpu.VMEM((seq, HEAD_W), F32)],
        compiler_params=_cparams(("parallel", "parallel")),
        name="hgrn2",
    )(hg3, hg3, hg3, hg3, hg3, lbs, gains, mlow, mhigh)


def _merge_kernel(a_ref, b_ref, gate_ref, x_ref, wa_ref, wb_ref, wo_ref, g_ref, xo_ref, h_ref):
    ya = jnp.dot(a_ref[...], wa_ref[...], preferred_element_type=F32)
    yb = jnp.dot(b_ref[...], wb_ref[...], preferred_element_type=F32)
    merged = (gate_ref[:, :D_MODEL].astype(F32) * ya
              + gate_ref[:, D_MODEL:].astype(F32) * yb).astype(BF16)
    x_new = x_ref[...] + jnp.dot(merged, wo_ref[...], preferred_element_type=F32)
    xo_ref[...] = x_new
    h_ref[...] = _rms_scale(x_new, g_ref[...]).astype(h_ref.dtype)


def _merge(attn_o, hg_o, gates, x, wa, wb, wo, gains, layer, tm):
    m = x.shape[0]
    row = lambda i: (i, 0)
    return pl.pallas_call(
        _merge_kernel,
        out_shape=(jax.ShapeDtypeStruct((m, D_MODEL), F32),
                   jax.ShapeDtypeStruct((m, D_MODEL), BF16)),
        grid=(m // tm,),
        in_specs=[pl.BlockSpec((tm, MIX_W), row),
                  pl.BlockSpec((tm, MIX_W), row),
                  pl.BlockSpec((tm, GATE_W), row),
                  pl.BlockSpec((tm, D_MODEL), row),
                  _layer_spec((MIX_W, D_MODEL), layer),
                  _layer_spec((MIX_W, D_MODEL), layer),
                  _layer_spec((D_MODEL, D_MODEL), layer),
                  _layer_spec((1, D_MODEL), layer)],
        out_specs=(pl.BlockSpec((tm, D_MODEL), row),
                   pl.BlockSpec((tm, D_MODEL), row)),
        compiler_params=_cparams(("parallel",)),
        name="merge_out",
    )(attn_o, hg_o, gates, x, wa, wb, wo, gains)


def _ffn_kernel(h_ref, x_ref, wg_ref, wu_ref, wd_ref, g_ref, *out_refs, emit_x):
    act_scr = out_refs[-1]
    h = h_ref[...]
    for c in range(FFN_HIDDEN // MXU_N):
        sl = slice(c * MXU_N, (c + 1) * MXU_N)
        gate = jnp.dot(h, wg_ref[:, sl], preferred_element_type=F32)
        up = jnp.dot(h, wu_ref[:, sl], preferred_element_type=F32)
        act_scr[:, sl] = (gate * _sigmoid(gate) * up).astype(BF16)
    x_new = x_ref[...] + jnp.dot(act_scr[...], wd_ref[...], preferred_element_type=F32)
    normed = _rms_scale(x_new, g_ref[...])
    if emit_x:
        out_refs[0][...] = x_new
        out_refs[1][...] = normed.astype(out_refs[1].dtype)
    else:
        out_refs[0][...] = normed.astype(out_refs[0].dtype)


def _ffn(h, x, wg, wu, wd, layer, next_gains, next_layer, tm, last):
    m = x.shape[0]
    row = lambda i: (i, 0)
    if last:
        out_shape = jax.ShapeDtypeStruct((m, D_MODEL), F32)
        out_specs = pl.BlockSpec((tm, D_MODEL), row)
    else:
        out_shape = (jax.ShapeDtypeStruct((m, D_MODEL), F32),
                     jax.ShapeDtypeStruct((m, D_MODEL), BF16))
        out_specs = (pl.BlockSpec((tm, D_MODEL), row), pl.BlockSpec((tm, D_MODEL), row))
    return pl.pallas_call(
        functools.partial(_ffn_kernel, emit_x=not last),
        out_shape=out_shape,
        grid=(m // tm,),
        in_specs=[pl.BlockSpec((tm, D_MODEL), row),
                  pl.BlockSpec((tm, D_MODEL), row),
                  _layer_spec((D_MODEL, FFN_HIDDEN), layer),
                  _layer_spec((D_MODEL, FFN_HIDDEN), layer),
                  _layer_spec((FFN_HIDDEN, D_MODEL), layer),
                  _layer_spec((1, D_MODEL), next_layer)],
        out_specs=out_specs,
        scratch_shapes=[pltpu.VMEM((tm, FFN_HIDDEN), BF16)],
        compiler_params=_cparams(("parallel",)),
        name="ffn",
    )(h, x, wg, wu, wd, next_gains)


def _rope_expand():
    half = ROT_DIM // 2
    e = np.zeros((2 * half, 3 * HEAD_W), np.float32)
    base = np.zeros((1, 3 * HEAD_W), np.float32)
    for l in range(HEAD_W):
        d = l % QK_DIM
        if d < half:
            e[d, l] = 1.0
            e[half + d, HEAD_W + l] = -1.0
        elif d < ROT_DIM:
            e[d - half, l] = 1.0
            e[half + d - half, 2 * HEAD_W + l] = 1.0
        else:
            base[0, l] = 1.0
    return e, base


def _rope_tables(positions):
    inv_freq = ROPE_THETA ** (-(jnp.arange(0, ROT_DIM, 2, dtype=F32) / ROT_DIM))
    ang = positions.astype(F32).reshape(-1, 1) * inv_freq
    cs = jnp.concatenate([jnp.cos(ang), jnp.sin(ang)], axis=1)
    e, base = _rope_expand()
    return jnp.dot(cs, jnp.asarray(e), precision=lax.Precision.HIGHEST) + jnp.asarray(base)


def kernel(x, positions, w_in, da_lambda, da_norm, hg_lb_logits, hg_norm, w_a, w_b, w_o,
           attn_norm, ffn_norm, w_gate, w_up, w_down, final_norm):
    batch, seq, _ = x.shape
    depth = w_in.shape[0]
    m = batch * seq
    tm = min(512, seq)
    tq = tk = tm

    rope_t = _rope_tables(positions)
    p = jax.nn.softmax(hg_lb_logits.astype(F32), axis=1)
    csum = jnp.cumsum(p, axis=1)
    lbs = csum - csum[:, :1]

    bf = lambda w: w.astype(BF16)
    w_in, w_a, w_b, w_o, w_gate, w_up, w_down = map(bf, (w_in, w_a, w_b, w_o, w_gate, w_up, w_down))
    attn_gains = attn_norm.astype(F32).reshape(depth, 1, D_MODEL)
    ffn_gains = ffn_norm.astype(F32).reshape(depth, 1, D_MODEL)
    final_gain = final_norm.astype(F32).reshape(1, 1, D_MODEL)
    da_gains = da_norm.astype(F32).reshape(depth, N_HEADS, 1, HEAD_W)
    hg_gains = hg_norm.astype(F32).reshape(depth, N_HEADS, 1, HEAD_W)

    xf = x.reshape(m, D_MODEL).astype(F32)
    h = _rmsnorm(xf, attn_gains, tm)
    out = None
    for layer in range(depth):
        lam_init = 0.8 - 0.6 * math.exp(-0.3 * layer)
        l32 = da_lambda[layer].astype(F32)
        lam_full = (jnp.exp(jnp.sum(l32[0] * l32[1])) - jnp.exp(jnp.sum(l32[2] * l32[3]))
                    + lam_init)
        lam = lam_full.reshape(1).astype(F32)
        lb_layer = lbs[:, layer].reshape(2, N_HEADS, HEAD_W).transpose(1, 0, 2)

        qk, vt, hg, gates = _in_proj(h, w_in, layer, rope_t, tm)
        attn_o = _diff_attention(qk, vt, lam, da_gains, layer, batch, seq, tq, tk, 1.0 - lam_init)
        hg_o = _hgrn2(hg, lb_layer, hg_gains, layer, batch, seq, min(512, seq))
        xf, h2 = _merge(attn_o.reshape(m, MIX_W), hg_o.reshape(m, MIX_W), gates, xf,
                        w_a, w_b, w_o, ffn_gains, layer, tm)
        last = layer == depth - 1
        if last:
            out = _ffn(h2, xf, w_gate, w_up, w_down, layer, final_gain, 0, tm, True)
        else:
            xf, h = _ffn(h2, xf, w_gate, w_up, w_down, layer, attn_gains, layer + 1, tm, False)
    return out.reshape(batch, seq, D_MODEL).astype(x.dtype)
```

```python
import functools
import math

import numpy as np
import jax
import jax.numpy as jnp
from jax import lax
from jax.experimental import pallas as pl
from jax.experimental.pallas import tpu as pltpu

F32 = jnp.float32
BF16 = jnp.bfloat16

D_MODEL = 1024
N_HEADS = 4
HEAD_W = 128
QK_DIM = 64
ROT_DIM = 16
ROPE_THETA = 500000.0
FFN_HIDDEN = 2816
NORM_EPS = 1e-6
MIX_W = N_HEADS * HEAD_W
HG_W = 5 * MIX_W
GATE_W = 2 * D_MODEL
IN_WIDTH = 3 * MIX_W + HG_W + GATE_W

LANES = 128
SUBLANES = 8
MXU_N = 256
VMEM_LIMIT = 56 * 1024 * 1024

HG_CHUNK = 128
HG_LEVELS = HG_CHUNK.bit_length() - 1
HG_LOW_LEVELS = SUBLANES.bit_length() - 1
HG_TILES = HG_CHUNK // SUBLANES

NT_DIMS = (((1,), (1,)), ((), ()))
Q_SCALE = QK_DIM ** -0.5 * math.log2(math.e)
ONES_ROWS = 16


def _cparams(sem):
    return pltpu.CompilerParams(dimension_semantics=sem, vmem_limit_bytes=VMEM_LIMIT)


def _sigmoid(x):
    return 1.0 / (1.0 + jnp.exp(-x))


def _rms_scale(x, gain):
    ms = jnp.mean(x * x, axis=-1, keepdims=True)
    return x * lax.rsqrt(ms + NORM_EPS) * gain


def _layer_spec(shape, layer):
    zeros = (0,) * len(shape)
    return pl.BlockSpec((None,) + tuple(shape), lambda *_: (layer,) + zeros)


def _resident_spec(shape, layer):
    zeros = (0,) * len(shape)
    return pl.BlockSpec((None,) + tuple(shape), lambda *_: (layer,) + zeros,
                        pipeline_mode=pl.Buffered(1))


def _proj_kernel(h_ref, w_ref, rope_ref, *rest, norm_input):
    if norm_input:
        g_ref, qk_ref, vt_ref, hg_ref, gate_ref = rest
        h = _rms_scale(h_ref[...], g_ref[...]).astype(BF16)
    else:
        qk_ref, vt_ref, hg_ref, gate_ref = rest
        h = h_ref[...]
    cos_t = rope_ref[:, 0:HEAD_W]
    sin_a = rope_ref[:, HEAD_W:2 * HEAD_W]
    sin_b = rope_ref[:, 2 * HEAD_W:3 * HEAD_W]

    def rope(a, scale):
        outs = []
        for hh in range(N_HEADS):
            t = a[:, hh * HEAD_W:(hh + 1) * HEAD_W]
            r = (t * cos_t + pltpu.roll(t, HEAD_W - ROT_DIM // 2, 1) * sin_a
                 + pltpu.roll(t, ROT_DIM // 2, 1) * sin_b)
            outs.append(r * scale if scale != 1.0 else r)
        return jnp.concatenate(outs, axis=1)

    n_groups = IN_WIDTH // MIX_W
    for g in range(n_groups):
        acc = jnp.dot(h, w_ref[:, g * MIX_W:(g + 1) * MIX_W], preferred_element_type=F32)
        if g == 0:
            qk_ref[:, 0:MIX_W] = rope(acc, Q_SCALE).astype(BF16)
        elif g == 1:
            qk_ref[:, MIX_W:2 * MIX_W] = rope(acc, 1.0).astype(BF16)
        elif g == 2:
            vt_ref[0] = acc.T.astype(BF16)
        elif g < 8:
            hg_ref[:, (g - 3) * MIX_W:(g - 2) * MIX_W] = acc
        else:
            gate_ref[:, (g - 8) * MIX_W:(g - 7) * MIX_W] = _sigmoid(acc).astype(BF16)


def _in_proj(h, w_all, layer, rope_t, tm, gains=None):
    m = h.shape[0]
    row = lambda i: (i, 0)
    extra_specs = [] if gains is None else [_layer_spec((1, D_MODEL), layer)]
    extra_args = [] if gains is None else [gains]
    return pl.pallas_call(
        functools.partial(_proj_kernel, norm_input=gains is not None),
        out_shape=(jax.ShapeDtypeStruct((m, 2 * MIX_W), BF16),
                   jax.ShapeDtypeStruct((m // tm, MIX_W, tm), BF16),
                   jax.ShapeDtypeStruct((m, HG_W), F32),
                   jax.ShapeDtypeStruct((m, GATE_W), BF16)),
        grid=(m // tm,),
        in_specs=[pl.BlockSpec((tm, D_MODEL), row),
                  _resident_spec((D_MODEL, IN_WIDTH), layer),
                  pl.BlockSpec((tm, 3 * HEAD_W), row)] + extra_specs,
        out_specs=(pl.BlockSpec((tm, 2 * MIX_W), row),
                   pl.BlockSpec((1, MIX_W, tm), lambda i: (i, 0, 0)),
                   pl.BlockSpec((tm, HG_W), row),
                   pl.BlockSpec((tm, GATE_W), row)),
        compiler_params=_cparams(("parallel",)),
        name="in_proj",
    )(h, w_all, rope_t, *extra_args)


def _attn_kernel(lam_ref, q_ref, k_ref, vt_ref, g_ref, o_ref, s_scr, acc_scr, fin_scr,
                 *, tq, tk, out_scale):
    seq = k_ref.shape[1]
    nk = seq // tk
    nq = seq // tq
    lane = lax.broadcasted_iota(jnp.int32, (tq, HEAD_W), 1)
    ones = jnp.ones((ONES_ROWS, tk), BF16)
    m0 = jnp.full((1, 2 * tq), -jnp.inf, F32)
    lam = lam_ref[0]
    gain = g_ref[0]

    def logits(u, j, slot):
        q = q_ref[0, pl.ds(pl.multiple_of(u * tq, tq), tq), :]
        zero = jnp.zeros_like(q)
        qs = jnp.concatenate([jnp.where(lane < QK_DIM, q, zero),
                              jnp.where(lane >= QK_DIM, q, zero)], axis=0)
        st = lax.dot_general(k_ref[0, pl.ds(pl.multiple_of(j * tk, tk), tk), :], qs, NT_DIMS,
                             preferred_element_type=F32)
        s_scr[slot] = st
        return jnp.max(st, axis=0, keepdims=True)

    def accumulate(j, slot, m_old, mx):
        m_new = jnp.maximum(m_old, mx)
        alpha = jnp.exp2(m_old - m_new)
        p = jnp.exp2(s_scr[slot] - m_new).astype(BF16)
        vt_ext = jnp.concatenate([vt_ref[j], ones], axis=0)
        acc_scr[...] = acc_scr[...] * alpha + jnp.dot(vt_ext, p, preferred_element_type=F32)
        return m_new

    def pair(jj, carry, u):
        m, mx = carry
        j = 2 * jj
        mx1 = logits(u, j + 1, 1)
        m = accumulate(j, 0, m, mx)
        mx2 = logits(u, j + 2, 0)
        m = accumulate(j + 1, 1, m, mx1)
        return m, mx2

    def finalize(u):
        acc = fin_scr[...]
        o_t = acc[:HEAD_W, :] / acc[HEAD_W:HEAD_W + 1, :]
        d_t = o_t[:, :tq] - lam * o_t[:, tq:]
        o_ref[0, pl.ds(pl.multiple_of(u * tq, tq), tq), :] = (
            _rms_scale(d_t.T, gain) * out_scale).astype(o_ref.dtype)

    def tile(u, mx):
        finalize(jnp.maximum(u - 1, 0))
        m, mx = lax.fori_loop(0, nk // 2 - 1, functools.partial(pair, u=u), (m0, mx), unroll=True)
        mx1 = logits(u, nk - 1, 1)
        m = accumulate(nk - 2, 0, m, mx)
        mx_next = logits(jnp.minimum(u + 1, nq - 1), 0, 0)
        accumulate(nk - 1, 1, m, mx1)
        fin_scr[...] = acc_scr[...]
        return mx_next

    acc_scr[...] = jnp.zeros(acc_scr.shape, F32)
    fin_scr[...] = jnp.ones(fin_scr.shape, F32)
    lax.fori_loop(0, nq, tile, logits(0, 0, 0))
    finalize(nq - 1)


def _diff_attention(qk, vt, lam, gains, layer, batch, seq, tq, tk, out_scale):
    qk3 = qk.reshape(batch, seq, 2 * MIX_W)
    nk = seq // tk
    assert nk % 2 == 0, "the attention pipeline walks key chunks in pairs"
    kern = functools.partial(_attn_kernel, tq=tq, tk=tk, out_scale=out_scale)
    return pl.pallas_call(
        kern,
        out_shape=jax.ShapeDtypeStruct((batch, seq, MIX_W), BF16),
        grid=(batch, N_HEADS),
        in_specs=[pl.BlockSpec(memory_space=pltpu.SMEM),
                  pl.BlockSpec((1, seq, HEAD_W), lambda b, h: (b, 0, h)),
                  pl.BlockSpec((1, seq, HEAD_W), lambda b, h: (b, 0, N_HEADS + h)),
                  pl.BlockSpec((nk, HEAD_W, tk), lambda b, h: (b, h, 0)),
                  pl.BlockSpec((None, 1, 1, HEAD_W), lambda b, h: (layer, h, 0, 0))],
        out_specs=pl.BlockSpec((1, seq, HEAD_W), lambda b, h: (b, 0, h)),
        scratch_shapes=[pltpu.VMEM((2, tk, 2 * tq), F32),
                        pltpu.VMEM((HEAD_W + ONES_ROWS, 2 * tq), F32),
                        pltpu.VMEM((HEAD_W + ONES_ROWS, 2 * tq), F32)],
        compiler_params=_cparams(("parallel", "parallel")),
        name="diff_attn",
    )(lam, qk3, qk3, vt, gains)


def _hg_masks():
    c = HG_CHUNK
    t = np.arange(c)[:, None]
    s = np.arange(c)[None, :]
    full = []
    for lvl in range(HG_LEVELS):
        h = 1 << lvl
        same = (t // (2 * h)) == (s // (2 * h))
        full.append(same & ((t & h) != 0) & ((s & h) == 0))
    low = ([full[l] for l in range(HG_LOW_LEVELS)] + [full[l].T for l in range(HG_LOW_LEVELS)]
           + [t == s])
    high = []
    for rev in (False, True):
        for lvl in range(HG_LOW_LEVELS, HG_LEVELS):
            h = 1 << lvl
            rows = np.arange(c)
            qrows = rows[(rows & h) == 0] if rev else rows[(rows & h) != 0]
            mk = full[lvl].T if rev else full[lvl]
            high.append(mk[qrows])
    return np.stack(low).astype(np.float32), np.stack(high).astype(np.float32)


def _hg_chunk(q, z, v, st_t, lb, mlow_ref, mhigh_ref, rev):
    c = HG_CHUNK
    oml = 1.0 - lb
    ez = jnp.exp(-jnp.abs(z))
    r = 1.0 / (1.0 + ez)
    er = ez * r
    pos = z >= 0.0
    f = lb + oml * jnp.where(pos, r, er)
    k = oml * jnp.where(pos, er, r)
    row = lax.broadcasted_iota(jnp.int32, (c, HEAD_W), 0)
    a_in = f
    e_ex = jnp.ones_like(f)
    tot = f
    sc = jnp.sum(q * k, axis=1, keepdims=True) * mlow_ref[2 * HG_LOW_LEVELS]
    for lvl in range(HG_LOW_LEVELS):
        h = 1 << lvl
        bit = (row & h) != 0
        qside = jnp.logical_not(bit) if rev else bit
        x = jnp.where(qside, q * a_in, k * e_ex).astype(BF16)
        lv = lax.dot_general(x, x, NT_DIMS, preferred_element_type=F32)
        sc = sc + lv * mlow_ref[(HG_LOW_LEVELS if rev else 0) + lvl]
        tot3 = tot.reshape(HG_TILES, SUBLANES, HEAD_W)
        prev = pltpu.roll(tot3, h, 1).reshape(c, HEAD_W)
        nxt = pltpu.roll(tot3, SUBLANES - h, 1).reshape(c, HEAD_W)
        if rev:
            a_in = a_in * jnp.where(bit, 1.0, nxt)
            e_ex = e_ex * jnp.where(bit, prev, 1.0)
        else:
            a_in = a_in * jnp.where(bit, prev, 1.0)
            e_ex = e_ex * jnp.where(bit, 1.0, nxt)
        tot = tot * jnp.where(bit, prev, nxt)

    tiles = lambda arr: [arr[i * SUBLANES:(i + 1) * SUBLANES] for i in range(HG_TILES)]
    q_t, k_t, a_t, e_t, sc_t = tiles(q), tiles(k), tiles(a_in), tiles(e_ex), tiles(sc)
    tb = tiles(tot)
    for lvl in range(HG_LOW_LEVELS, HG_LEVELS):
        ht = (1 << lvl) // SUBLANES
        n_blocks = HG_TILES // (2 * ht)
        x_t = [None] * HG_TILES
        q_idx = []
        for b in range(n_blocks):
            lo = range(2 * b * ht, (2 * b + 1) * ht)
            hi = range((2 * b + 1) * ht, (2 * b + 2) * ht)
            q_half, k_half = (lo, hi) if rev else (hi, lo)
            for i in q_half:
                x_t[i] = q_t[i] * a_t[i]
            for i in k_half:
                x_t[i] = k_t[i] * e_t[i]
            q_idx.extend(q_half)
        x = jnp.concatenate(x_t, axis=0).astype(BF16)
        xq = jnp.concatenate([x_t[i] for i in q_idx], axis=0).astype(BF16)
        lv = lax.dot_general(xq, x, NT_DIMS, preferred_element_type=F32)
        mk = mhigh_ref[(HG_LEVELS - HG_LOW_LEVELS if rev else 0) + lvl - HG_LOW_LEVELS]
        for n, i in enumerate(q_idx):
            rows = slice(n * SUBLANES, (n + 1) * SUBLANES)
            sc_t[i] = sc_t[i] + lv[rows] * mk[rows]
        new_tb = []
        for b in range(n_blocks):
            t_lo, t_hi = tb[2 * b], tb[2 * b + 1]
            lo = range(2 * b * ht, (2 * b + 1) * ht)
            hi = range((2 * b + 1) * ht, (2 * b + 2) * ht)
            if rev:
                for i in lo:
                    a_t[i] = a_t[i] * t_hi
                for i in hi:
                    e_t[i] = e_t[i] * t_lo
            else:
                for i in hi:
                    a_t[i] = a_t[i] * t_lo
                for i in lo:
                    e_t[i] = e_t[i] * t_hi
            new_tb.append(t_lo * t_hi)
        tb = new_tb

    sc = jnp.concatenate(sc_t, axis=0)
    qa = jnp.concatenate([q_t[i] * a_t[i] for i in range(HG_TILES)], axis=0)
    ke = jnp.concatenate([k_t[i] * e_t[i] for i in range(HG_TILES)], axis=0)
    vb = v.astype(BF16)
    intra = jnp.dot(sc.astype(BF16), vb, preferred_element_type=F32)
    inter = lax.dot_general(qa.astype(BF16), st_t.astype(BF16), NT_DIMS,
                            preferred_element_type=F32)
    upd = jnp.dot(v.T.astype(BF16), ke.astype(BF16), preferred_element_type=F32)
    st_t = st_t * tb[0][0:1, :] + upd
    return inter + intra, st_t


def _hgrn_kernel(q_ref, zf_ref, zb_ref, v_ref, g_ref, lb_ref, gain_ref, mlow_ref, mhigh_ref,
                 o_ref, of_scr, ob_scr, *, tr):
    seq = q_ref.shape[1]
    c = HG_CHUNK
    n = seq // c
    lb_f = lb_ref[0, 0:1, :]
    lb_b = lb_ref[0, 1:2, :]

    def body(i, carry):
        st_f, st_b = carry
        cf = pl.multiple_of(i * c, c)
        cb = pl.multiple_of((n - 1 - i) * c, c)
        o_f, st_f = _hg_chunk(q_ref[0, pl.ds(cf, c), :], zf_ref[0, pl.ds(cf, c), :],
                              v_ref[0, pl.ds(cf, c), :], st_f, lb_f, mlow_ref, mhigh_ref, False)
        of_scr[pl.ds(cf, c), :] = o_f
        o_b, st_b = _hg_chunk(q_ref[0, pl.ds(cb, c), :], zb_ref[0, pl.ds(cb, c), :],
                              v_ref[0, pl.ds(cb, c), :], st_b, lb_b, mlow_ref, mhigh_ref, True)
        ob_scr[pl.ds(cb, c), :] = o_b
        return st_f, st_b

    zero = jnp.zeros((HEAD_W, HEAD_W), F32)
    lax.fori_loop(0, n, body, (zero, zero), unroll=4 if n % 4 == 0 else 1)

    gain = gain_ref[0]

    def finish(i, carry):
        off = pl.multiple_of(i * tr, tr)
        o = of_scr[pl.ds(off, tr), :] + ob_scr[pl.ds(off, tr), :]
        y = _rms_scale(o, gain) * _sigmoid(g_ref[0, pl.ds(off, tr), :])
        o_ref[0, pl.ds(off, tr), :] = y.astype(o_ref.dtype)
        return carry

    lax.fori_loop(0, seq // tr, finish, 0)


def _hgrn2(hg, lbs, gains, layer, batch, seq, tr):
    hg3 = hg.reshape(batch, seq, HG_W)
    mlow, mhigh = (jnp.asarray(a) for a in _hg_masks())
    col = lambda g: (lambda b, h: (b, 0, g * N_HEADS + h))
    kern = functools.partial(_hgrn_kernel, tr=tr)
    return pl.pallas_call(
        kern,
        out_shape=jax.ShapeDtypeStruct((batch, seq, MIX_W), BF16),
        grid=(batch, N_HEADS),
        in_specs=[pl.BlockSpec((1, seq, HEAD_W), col(0)),
                  pl.BlockSpec((1, seq, HEAD_W), col(1)),
                  pl.BlockSpec((1, seq, HEAD_W), col(2)),
                  pl.BlockSpec((1, seq, HEAD_W), col(3)),
                  pl.BlockSpec((1, seq, HEAD_W), col(4)),
                  pl.BlockSpec((1, 2, HEAD_W), lambda b, h: (h, 0, 0)),
                  pl.BlockSpec((None, 1, 1, HEAD_W), lambda b, h: (layer, h, 0, 0)),
                  pl.BlockSpec(mlow.shape, lambda b, h: (0, 0, 0)),
                  pl.BlockSpec(mhigh.shape, lambda b, h: (0, 0, 0))],
        out_specs=pl.BlockSpec((1, seq, HEAD_W), lambda b, h: (b, 0, h)),
        scratch_shapes=[pltpu.VMEM((seq, HEAD_W), F32),
                        pltpu.VMEM((seq, HEAD_W), F32)],
        compiler_params=_cparams(("parallel", "parallel")),
        name="hgrn2",
    )(hg3, hg3, hg3, hg3, hg3, lbs, gains, mlow, mhigh)


def _post_kernel(a_ref, b_ref, gate_ref, x_ref, wa_ref, wb_ref, wo_ref, wg_ref, wu_ref, wd_ref,
                 g1_ref, g2_ref, *out_refs, emit_x):
    act_scr = out_refs[-1]
    ya = jnp.dot(a_ref[...], wa_ref[...], preferred_element_type=F32)
    yb = jnp.dot(b_ref[...], wb_ref[...], preferred_element_type=F32)
    merged = (gate_ref[:, :D_MODEL].astype(F32) * ya
              + gate_ref[:, D_MODEL:].astype(F32) * yb).astype(BF16)
    x_mid = x_ref[...] + jnp.dot(merged, wo_ref[...], preferred_element_type=F32)
    h = _rms_scale(x_mid, g1_ref[...]).astype(BF16)
    for c in range(FFN_HIDDEN // MXU_N):
        sl = slice(c * MXU_N, (c + 1) * MXU_N)
        gate = jnp.dot(h, wg_ref[:, sl], preferred_element_type=F32)
        up = jnp.dot(h, wu_ref[:, sl], preferred_element_type=F32)
        act_scr[:, sl] = (gate * _sigmoid(gate) * up).astype(BF16)
    x_new = x_mid + jnp.dot(act_scr[...], wd_ref[...], preferred_element_type=F32)
    normed = _rms_scale(x_new, g2_ref[...])
    if emit_x:
        out_refs[0][...] = x_new
        out_refs[1][...] = normed.astype(out_refs[1].dtype)
    else:
        out_refs[0][...] = normed.astype(out_refs[0].dtype)


def _post(attn_o, hg_o, gates, x, wa, wb, wo, wg, wu, wd, layer, ffn_gains, next_gains,
          next_layer, tm, last):
    m = x.shape[0]
    row = lambda i: (i, 0)
    if last:
        out_shape = jax.ShapeDtypeStruct((m, D_MODEL), F32)
        out_specs = pl.BlockSpec((tm, D_MODEL), row)
    else:
        out_shape = (jax.ShapeDtypeStruct((m, D_MODEL), F32),
                     jax.ShapeDtypeStruct((m, D_MODEL), BF16))
        out_specs = (pl.BlockSpec((tm, D_MODEL), row), pl.BlockSpec((tm, D_MODEL), row))
    return pl.pallas_call(
        functools.partial(_post_kernel, emit_x=not last),
        out_shape=out_shape,
        grid=(m // tm,),
        in_specs=[pl.BlockSpec((tm, MIX_W), row),
                  pl.BlockSpec((tm, MIX_W), row),
                  pl.BlockSpec((tm, GATE_W), row),
                  pl.BlockSpec((tm, D_MODEL), row),
                  _resident_spec((MIX_W, D_MODEL), layer),
                  _resident_spec((MIX_W, D_MODEL), layer),
                  _resident_spec((D_MODEL, D_MODEL), layer),
                  _resident_spec((D_MODEL, FFN_HIDDEN), layer),
                  _resident_spec((D_MODEL, FFN_HIDDEN), layer),
                  _resident_spec((FFN_HIDDEN, D_MODEL), layer),
                  _layer_spec((1, D_MODEL), layer),
                  _layer_spec((1, D_MODEL), next_layer)],
        out_specs=out_specs,
        scratch_shapes=[pltpu.VMEM((tm, FFN_HIDDEN), BF16)],
        compiler_params=_cparams(("parallel",)),
        name="post",
    )(attn_o, hg_o, gates, x, wa, wb, wo, wg, wu, wd, ffn_gains, next_gains)


def _rope_expand():
    half = ROT_DIM // 2
    e = np.zeros((2 * half, 3 * HEAD_W), np.float32)
    base = np.zeros((1, 3 * HEAD_W), np.float32)
    for l in range(HEAD_W):
        d = l % QK_DIM
        if d < half:
            e[d, l] = 1.0
            e[half + d, HEAD_W + l] = -1.0
        elif d < ROT_DIM:
            e[d - half, l] = 1.0
            e[half + d - half, 2 * HEAD_W + l] = 1.0
        else:
            base[0, l] = 1.0
    return e, base


def _rope_tables(positions):
    inv_freq = ROPE_THETA ** (-(jnp.arange(0, ROT_DIM, 2, dtype=F32) / ROT_DIM))
    ang = positions.astype(F32).reshape(-1, 1) * inv_freq
    cs = jnp.concatenate([jnp.cos(ang), jnp.sin(ang)], axis=1)
    e, base = _rope_expand()
    return jnp.dot(cs, jnp.asarray(e), precision=lax.Precision.HIGHEST) + jnp.asarray(base)


def kernel(x, positions, w_in, da_lambda, da_norm, hg_lb_logits, hg_norm, w_a, w_b, w_o,
           attn_norm, ffn_norm, w_gate, w_up, w_down, final_norm):
    batch, seq, _ = x.shape
    depth = w_in.shape[0]
    m = batch * seq
    tm = min(512, seq)
    tq = tk = tm

    rope_t = _rope_tables(positions)
    p = jax.nn.softmax(hg_lb_logits.astype(F32), axis=1)
    csum = jnp.cumsum(p, axis=1)
    lbs = csum - csum[:, :1]

    bf = lambda w: w.astype(BF16)
    w_in, w_a, w_b, w_o, w_gate, w_up, w_down = map(bf, (w_in, w_a, w_b, w_o, w_gate, w_up, w_down))
    attn_gains = attn_norm.astype(F32).reshape(depth, 1, D_MODEL)
    ffn_gains = ffn_norm.astype(F32).reshape(depth, 1, D_MODEL)
    final_gain = final_norm.astype(F32).reshape(1, 1, D_MODEL)
    da_gains = da_norm.astype(F32).reshape(depth, N_HEADS, 1, HEAD_W)
    hg_gains = hg_norm.astype(F32).reshape(depth, N_HEADS, 1, HEAD_W)

    xf = x.reshape(m, D_MODEL).astype(F32)
    h = None
    out = None
    for layer in range(depth):
        lam_init = 0.8 - 0.6 * math.exp(-0.3 * layer)
        l32 = da_lambda[layer].astype(F32)
        lam_full = (jnp.exp(jnp.sum(l32[0] * l32[1])) - jnp.exp(jnp.sum(l32[2] * l32[3]))
                    + lam_init)
        lam = lam_full.reshape(1).astype(F32)
        lb_layer = lbs[:, layer].reshape(2, N_HEADS, HEAD_W).transpose(1, 0, 2)

        if layer == 0:
            qk, vt, hg, gates = _in_proj(xf, w_in, layer, rope_t, tm, gains=attn_gains)
        else:
            qk, vt, hg, gates = _in_proj(h, w_in, layer, rope_t, tm)
        attn_o = _diff_attention(qk, vt, lam, da_gains, layer, batch, seq, tq, tk, 1.0 - lam_init)
        hg_o = _hgrn2(hg, lb_layer, hg_gains, layer, batch, seq, min(512, seq))
        last = layer == depth - 1
        args = (attn_o.reshape(m, MIX_W), hg_o.reshape(m, MIX_W), gates, xf,
                w_a, w_b, w_o, w_gate, w_up, w_down, layer, ffn_gains)
        if last:
            out = _post(*args, final_gain, 0, tm, True)
        else:
            xf, h = _post(*args, attn_gains, layer + 1, tm, False)
    return out.reshape(batch, seq, D_MODEL).astype(x.dtype)
```

```python
import functools
import math

import numpy as np
import jax
import jax.numpy as jnp
from jax import lax
from jax.experimental import pallas as pl
from jax.experimental.pallas import tpu as pltpu

F32 = jnp.float32
BF16 = jnp.bfloat16

D_MODEL = 1024
N_HEADS = 4
HEAD_W = 128
QK_DIM = 64
ROT_DIM = 16
ROPE_THETA = 500000.0
FFN_HIDDEN = 2816
NORM_EPS = 1e-6
MIX_W = N_HEADS * HEAD_W
HG_W = 5 * MIX_W
GATE_W = 2 * D_MODEL
IN_WIDTH = 3 * MIX_W + HG_W + GATE_W

LANES = 128
SUBLANES = 8
MXU_N = 256
VMEM_LIMIT = 56 * 1024 * 1024

HG_CHUNK = 128
HG_LEVELS = HG_CHUNK.bit_length() - 1
HG_LOW_LEVELS = SUBLANES.bit_length() - 1
HG_TILES = HG_CHUNK // SUBLANES

NT_DIMS = (((1,), (1,)), ((), ()))
Q_SCALE = QK_DIM ** -0.5 * math.log2(math.e)
ONES_ROWS = 16


def _cparams(sem):
    return pltpu.CompilerParams(dimension_semantics=sem, vmem_limit_bytes=VMEM_LIMIT)


def _sigmoid(x):
    return 1.0 / (1.0 + jnp.exp(-x))


def _rms_scale(x, gain):
    ms = jnp.mean(x * x, axis=-1, keepdims=True)
    return x * lax.rsqrt(ms + NORM_EPS) * gain


def _layer_spec(shape, layer):
    zeros = (0,) * len(shape)
    return pl.BlockSpec((None,) + tuple(shape), lambda *_: (layer,) + zeros)


def _resident_spec(shape, layer):
    zeros = (0,) * len(shape)
    return pl.BlockSpec((None,) + tuple(shape), lambda *_: (layer,) + zeros,
                        pipeline_mode=pl.Buffered(1))


def _proj_kernel(h_ref, w_ref, rope_ref, *rest, norm_input):
    if norm_input:
        g_ref, qk_ref, vt_ref, hg_ref, gate_ref = rest
        h = _rms_scale(h_ref[...], g_ref[...]).astype(BF16)
    else:
        qk_ref, vt_ref, hg_ref, gate_ref = rest
        h = h_ref[...]
    cos_t = rope_ref[:, 0:HEAD_W]
    sin_a = rope_ref[:, HEAD_W:2 * HEAD_W]
    sin_b = rope_ref[:, 2 * HEAD_W:3 * HEAD_W]

    def rope(a, scale):
        outs = []
        for hh in range(N_HEADS):
            t = a[:, hh * HEAD_W:(hh + 1) * HEAD_W]
            r = (t * cos_t + pltpu.roll(t, HEAD_W - ROT_DIM // 2, 1) * sin_a
                 + pltpu.roll(t, ROT_DIM // 2, 1) * sin_b)
            outs.append(r * scale if scale != 1.0 else r)
        return jnp.concatenate(outs, axis=1)

    n_groups = IN_WIDTH // MIX_W
    for g in range(n_groups):
        acc = jnp.dot(h, w_ref[:, g * MIX_W:(g + 1) * MIX_W], preferred_element_type=F32)
        if g == 0:
            qk_ref[:, 0:MIX_W] = rope(acc, Q_SCALE).astype(BF16)
        elif g == 1:
            qk_ref[:, MIX_W:2 * MIX_W] = rope(acc, 1.0).astype(BF16)
        elif g == 2:
            vt_ref[0] = acc.T.astype(BF16)
        elif g < 8:
            hg_ref[:, (g - 3) * MIX_W:(g - 2) * MIX_W] = acc
        else:
            gate_ref[:, (g - 8) * MIX_W:(g - 7) * MIX_W] = _sigmoid(acc).astype(BF16)


def _in_proj(h, w_all, layer, rope_t, tm, gains=None):
    m = h.shape[0]
    row = lambda i: (i, 0)
    extra_specs = [] if gains is None else [_layer_spec((1, D_MODEL), layer)]
    extra_args = [] if gains is None else [gains]
    return pl.pallas_call(
        functools.partial(_proj_kernel, norm_input=gains is not None),
        out_shape=(jax.ShapeDtypeStruct((m, 2 * MIX_W), BF16),
                   jax.ShapeDtypeStruct((m // tm, MIX_W, tm), BF16),
                   jax.ShapeDtypeStruct((m, HG_W), F32),
                   jax.ShapeDtypeStruct((m, GATE_W), BF16)),
        grid=(m // tm,),
        in_specs=[pl.BlockSpec((tm, D_MODEL), row),
                  _resident_spec((D_MODEL, IN_WIDTH), layer),
                  pl.BlockSpec((tm, 3 * HEAD_W), row)] + extra_specs,
        out_specs=(pl.BlockSpec((tm, 2 * MIX_W), row),
                   pl.BlockSpec((1, MIX_W, tm), lambda i: (i, 0, 0)),
                   pl.BlockSpec((tm, HG_W), row),
                   pl.BlockSpec((tm, GATE_W), row)),
        compiler_params=_cparams(("parallel",)),
        name="in_proj",
    )(h, w_all, rope_t, *extra_args)


def _attn_body(lam_ref, q_ref, k_ref, vt_ref, g_ref, o_ref, s_scr, acc_scr, fin_scr,
               *, tq, tk, out_scale, side_work):
    seq = k_ref.shape[1]
    nk = seq // tk
    nq = seq // tq
    lane = lax.broadcasted_iota(jnp.int32, (tq, HEAD_W), 1)
    ones = jnp.ones((ONES_ROWS, tk), BF16)
    m0 = jnp.full((1, 2 * tq), -jnp.inf, F32)
    lam = lam_ref[0]
    gain = g_ref[0]

    def logits(u, j, slot):
        q = q_ref[0, pl.ds(pl.multiple_of(u * tq, tq), tq), :]
        zero = jnp.zeros_like(q)
        qs = jnp.concatenate([jnp.where(lane < QK_DIM, q, zero),
                              jnp.where(lane >= QK_DIM, q, zero)], axis=0)
        st = lax.dot_general(k_ref[0, pl.ds(pl.multiple_of(j * tk, tk), tk), :], qs, NT_DIMS,
                             preferred_element_type=F32)
        s_scr[slot] = st
        return jnp.max(st, axis=0, keepdims=True)

    def accumulate(j, slot, m_old, mx):
        m_new = jnp.maximum(m_old, mx)
        alpha = jnp.exp2(m_old - m_new)
        p = jnp.exp2(s_scr[slot] - m_new).astype(BF16)
        vt_ext = jnp.concatenate([vt_ref[j], ones], axis=0)
        acc_scr[...] = acc_scr[...] * alpha + jnp.dot(vt_ext, p, preferred_element_type=F32)
        return m_new

    def pair(jj, carry, u):
        m, mx = carry
        j = 2 * jj
        mx1 = logits(u, j + 1, 1)
        m = accumulate(j, 0, m, mx)
        mx2 = logits(u, j + 2, 0)
        m = accumulate(j + 1, 1, m, mx1)
        return m, mx2

    def finalize(u):
        acc = fin_scr[...]
        o_t = acc[:HEAD_W, :] / acc[HEAD_W:HEAD_W + 1, :]
        d_t = o_t[:, :tq] - lam * o_t[:, tq:]
        o_ref[0, pl.ds(pl.multiple_of(u * tq, tq), tq), :] = (
            _rms_scale(d_t.T, gain) * out_scale).astype(o_ref.dtype)

    def tile(u, mx):
        finalize(jnp.maximum(u - 1, 0))
        side_work(u)
        m, mx = lax.fori_loop(0, nk // 2 - 1, functools.partial(pair, u=u), (m0, mx), unroll=True)
        mx1 = logits(u, nk - 1, 1)
        m = accumulate(nk - 2, 0, m, mx)
        mx_next = logits(jnp.minimum(u + 1, nq - 1), 0, 0)
        accumulate(nk - 1, 1, m, mx1)
        fin_scr[...] = acc_scr[...]
        return mx_next

    acc_scr[...] = jnp.zeros(acc_scr.shape, F32)
    fin_scr[...] = jnp.ones(fin_scr.shape, F32)
    lax.fori_loop(0, nq, tile, logits(0, 0, 0))
    finalize(nq - 1)


def _hg_masks():
    c = HG_CHUNK
    t = np.arange(c)[:, None]
    s = np.arange(c)[None, :]
    full = []
    for lvl in range(HG_LEVELS):
        h = 1 << lvl
        same = (t // (2 * h)) == (s // (2 * h))
        full.append(same & ((t & h) != 0) & ((s & h) == 0))
    low = ([full[l] for l in range(HG_LOW_LEVELS)] + [full[l].T for l in range(HG_LOW_LEVELS)]
           + [t == s])
    high = []
    for rev in (False, True):
        for lvl in range(HG_LOW_LEVELS, HG_LEVELS):
            h = 1 << lvl
            rows = np.arange(c)
            qrows = rows[(rows & h) == 0] if rev else rows[(rows & h) != 0]
            mk = full[lvl].T if rev else full[lvl]
            high.append(mk[qrows])
    return np.stack(low).astype(np.float32), np.stack(high).astype(np.float32)


def _hg_chunk(q, z, v, st_t, lb, mlow_ref, mhigh_ref, rev):
    c = HG_CHUNK
    oml = 1.0 - lb
    ez = jnp.exp(-jnp.abs(z))
    r = 1.0 / (1.0 + ez)
    er = ez * r
    pos = z >= 0.0
    f = lb + oml * jnp.where(pos, r, er)
    k = oml * jnp.where(pos, er, r)
    row = lax.broadcasted_iota(jnp.int32, (c, HEAD_W), 0)
    a_in = f
    e_ex = jnp.ones_like(f)
    tot = f
    sc = jnp.sum(q * k, axis=1, keepdims=True) * mlow_ref[2 * HG_LOW_LEVELS]
    for lvl in range(HG_LOW_LEVELS):
        h = 1 << lvl
        bit = (row & h) != 0
        qside = jnp.logical_not(bit) if rev else bit
        x = jnp.where(qside, q * a_in, k * e_ex).astype(BF16)
        lv = lax.dot_general(x, x, NT_DIMS, preferred_element_type=F32)
        sc = sc + lv * mlow_ref[(HG_LOW_LEVELS if rev else 0) + lvl]
        tot3 = tot.reshape(HG_TILES, SUBLANES, HEAD_W)
        prev = pltpu.roll(tot3, h, 1).reshape(c, HEAD_W)
        nxt = pltpu.roll(tot3, SUBLANES - h, 1).reshape(c, HEAD_W)
        if rev:
            a_in = a_in * jnp.where(bit, 1.0, nxt)
            e_ex = e_ex * jnp.where(bit, prev, 1.0)
        else:
            a_in = a_in * jnp.where(bit, prev, 1.0)
            e_ex = e_ex * jnp.where(bit, 1.0, nxt)
        tot = tot * jnp.where(bit, prev, nxt)

    tiles = lambda arr: [arr[i * SUBLANES:(i + 1) * SUBLANES] for i in range(HG_TILES)]
    q_t, k_t, a_t, e_t, sc_t = tiles(q), tiles(k), tiles(a_in), tiles(e_ex), tiles(sc)
    tb = tiles(tot)
    for lvl in range(HG_LOW_LEVELS, HG_LEVELS):
        ht = (1 << lvl) // SUBLANES
        n_blocks = HG_TILES // (2 * ht)
        x_t = [None] * HG_TILES
        q_idx = []
        for b in range(n_blocks):
            lo = range(2 * b * ht, (2 * b + 1) * ht)
            hi = range((2 * b + 1) * ht, (2 * b + 2) * ht)
            q_half, k_half = (lo, hi) if rev else (hi, lo)
            for i in q_half:
                x_t[i] = q_t[i] * a_t[i]
            for i in k_half:
                x_t[i] = k_t[i] * e_t[i]
            q_idx.extend(q_half)
        x = jnp.concatenate(x_t, axis=0).astype(BF16)
        xq = jnp.concatenate([x_t[i] for i in q_idx], axis=0).astype(BF16)
        lv = lax.dot_general(xq, x, NT_DIMS, preferred_element_type=F32)
        mk = mhigh_ref[(HG_LEVELS - HG_LOW_LEVELS if rev else 0) + lvl - HG_LOW_LEVELS]
        for n, i in enumerate(q_idx):
            rows = slice(n * SUBLANES, (n + 1) * SUBLANES)
            sc_t[i] = sc_t[i] + lv[rows] * mk[rows]
        new_tb = []
        for b in range(n_blocks):
            t_lo, t_hi = tb[2 * b], tb[2 * b + 1]
            lo = range(2 * b * ht, (2 * b + 1) * ht)
            hi = range((2 * b + 1) * ht, (2 * b + 2) * ht)
            if rev:
                for i in lo:
                    a_t[i] = a_t[i] * t_hi
                for i in hi:
                    e_t[i] = e_t[i] * t_lo
            else:
                for i in hi:
                    a_t[i] = a_t[i] * t_lo
                for i in lo:
                    e_t[i] = e_t[i] * t_hi
            new_tb.append(t_lo * t_hi)
        tb = new_tb

    sc = jnp.concatenate(sc_t, axis=0)
    qa = jnp.concatenate([q_t[i] * a_t[i] for i in range(HG_TILES)], axis=0)
    ke = jnp.concatenate([k_t[i] * e_t[i] for i in range(HG_TILES)], axis=0)
    vb = v.astype(BF16)
    intra = jnp.dot(sc.astype(BF16), vb, preferred_element_type=F32)
    inter = lax.dot_general(qa.astype(BF16), st_t.astype(BF16), NT_DIMS,
                            preferred_element_type=F32)
    upd = jnp.dot(v.T.astype(BF16), ke.astype(BF16), preferred_element_type=F32)
    st_t = st_t * tb[0][0:1, :] + upd
    return inter + intra, st_t


def _mixer_kernel(lam_ref, q_ref, k_ref, vt_ref, ag_ref, hq_ref, zf_ref, zb_ref, hv_ref, hg_ref,
                  lb_ref, hgain_ref, mlow_ref, mhigh_ref, ao_ref, ho_ref,
                  s_scr, acc_scr, fin_scr, of_scr, ob_scr, st_scr, *, tq, tk, tr, out_scale):
    seq = q_ref.shape[1]
    c = HG_CHUNK
    n = seq // c
    per_tile = n // (seq // tq)
    lb_f = lb_ref[0, 0:1, :]
    lb_b = lb_ref[0, 1:2, :]
    st_scr[...] = jnp.zeros(st_scr.shape, F32)

    def scans(u):
        st_f = st_scr[0]
        st_b = st_scr[1]
        for r in range(per_tile):
            i = u * per_tile + r
            cf = pl.multiple_of(i * c, c)
            cb = pl.multiple_of((n - 1 - i) * c, c)
            o_f, st_f = _hg_chunk(hq_ref[0, pl.ds(cf, c), :], zf_ref[0, pl.ds(cf, c), :],
                                  hv_ref[0, pl.ds(cf, c), :], st_f, lb_f, mlow_ref, mhigh_ref,
                                  False)
            of_scr[pl.ds(cf, c), :] = o_f
            o_b, st_b = _hg_chunk(hq_ref[0, pl.ds(cb, c), :], zb_ref[0, pl.ds(cb, c), :],
                                  hv_ref[0, pl.ds(cb, c), :], st_b, lb_b, mlow_ref, mhigh_ref,
                                  True)
            ob_scr[pl.ds(cb, c), :] = o_b
        st_scr[0] = st_f
        st_scr[1] = st_b

    _attn_body(lam_ref, q_ref, k_ref, vt_ref, ag_ref, ao_ref, s_scr, acc_scr, fin_scr,
               tq=tq, tk=tk, out_scale=out_scale, side_work=scans)

    gain = hgain_ref[0]

    def finish(i, carry):
        off = pl.multiple_of(i * tr, tr)
        o = of_scr[pl.ds(off, tr), :] + ob_scr[pl.ds(off, tr), :]
        y = _rms_scale(o, gain) * _sigmoid(hg_ref[0, pl.ds(off, tr), :])
        ho_ref[0, pl.ds(off, tr), :] = y.astype(ho_ref.dtype)
        return carry

    lax.fori_loop(0, seq // tr, finish, 0)


def _mixers(qk, vt, hg, lam, lbs, da_gains, hg_gains, layer, batch, seq, tq, tk, tr, out_scale):
    qk3 = qk.reshape(batch, seq, 2 * MIX_W)
    hg3 = hg.reshape(batch, seq, HG_W)
    nk = seq // tk
    assert nk % 2 == 0, "the attention pipeline walks key chunks in pairs"
    assert (seq // HG_CHUNK) % (seq // tq) == 0, "whole scan chunks per query tile"
    mlow, mhigh = (jnp.asarray(a) for a in _hg_masks())
    col = lambda g: (lambda b, h: (b, 0, g * N_HEADS + h))
    head_gain = pl.BlockSpec((None, 1, 1, HEAD_W), lambda b, h: (layer, h, 0, 0))
    seq_block = lambda g: pl.BlockSpec((1, seq, HEAD_W), col(g))
    out = jax.ShapeDtypeStruct((batch, seq, MIX_W), BF16)
    kern = functools.partial(_mixer_kernel, tq=tq, tk=tk, tr=tr, out_scale=out_scale)
    return pl.pallas_call(
        kern,
        out_shape=(out, out),
        grid=(batch, N_HEADS),
        in_specs=[pl.BlockSpec(memory_space=pltpu.SMEM),
                  seq_block(0),
                  seq_block(1),
                  pl.BlockSpec((nk, HEAD_W, tk), lambda b, h: (b, h, 0)),
                  head_gain,
                  seq_block(0), seq_block(1), seq_block(2), seq_block(3), seq_block(4),
                  pl.BlockSpec((1, 2, HEAD_W), lambda b, h: (h, 0, 0)),
                  head_gain,
                  pl.BlockSpec(mlow.shape, lambda b, h: (0, 0, 0)),
                  pl.BlockSpec(mhigh.shape, lambda b, h: (0, 0, 0))],
        out_specs=(pl.BlockSpec((1, seq, HEAD_W), lambda b, h: (b, 0, h)),
                   pl.BlockSpec((1, seq, HEAD_W), lambda b, h: (b, 0, h))),
        scratch_shapes=[pltpu.VMEM((2, tk, 2 * tq), F32),
                        pltpu.VMEM((HEAD_W + ONES_ROWS, 2 * tq), F32),
                        pltpu.VMEM((HEAD_W + ONES_ROWS, 2 * tq), F32),
                        pltpu.VMEM((seq, HEAD_W), F32),
                        pltpu.VMEM((seq, HEAD_W), F32),
                        pltpu.VMEM((2, HEAD_W, HEAD_W), F32)],
        compiler_params=_cparams(("parallel", "parallel")),
        name="mixers",
    )(lam, qk3, qk3, vt, da_gains, hg3, hg3, hg3, hg3, hg3, lbs, hg_gains, mlow, mhigh)


def _post_kernel(a_ref, b_ref, gate_ref, x_ref, wa_ref, wb_ref, wo_ref, wg_ref, wu_ref, wd_ref,
                 g1_ref, g2_ref, *out_refs, emit_x):
    act_scr = out_refs[-1]
    ya = jnp.dot(a_ref[...], wa_ref[...], preferred_element_type=F32)
    yb = jnp.dot(b_ref[...], wb_ref[...], preferred_element_type=F32)
    merged = (gate_ref[:, :D_MODEL].astype(F32) * ya
              + gate_ref[:, D_MODEL:].astype(F32) * yb).astype(BF16)
    x_mid = x_ref[...] + jnp.dot(merged, wo_ref[...], preferred_element_type=F32)
    h = _rms_scale(x_mid, g1_ref[...]).astype(BF16)
    for c in range(FFN_HIDDEN // MXU_N):
        sl = slice(c * MXU_N, (c + 1) * MXU_N)
        gate = jnp.dot(h, wg_ref[:, sl], preferred_element_type=F32)
        up = jnp.dot(h, wu_ref[:, sl], preferred_element_type=F32)
        act_scr[:, sl] = (gate * _sigmoid(gate) * up).astype(BF16)
    x_new = x_mid + jnp.dot(act_scr[...], wd_ref[...], preferred_element_type=F32)
    normed = _rms_scale(x_new, g2_ref[...])
    if emit_x:
        out_refs[0][...] = x_new
        out_refs[1][...] = normed.astype(out_refs[1].dtype)
    else:
        out_refs[0][...] = normed.astype(out_refs[0].dtype)


def _post(attn_o, hg_o, gates, x, wa, wb, wo, wg, wu, wd, layer, ffn_gains, next_gains,
          next_layer, tm, last):
    m = x.shape[0]
    row = lambda i: (i, 0)
    if last:
        out_shape = jax.ShapeDtypeStruct((m, D_MODEL), F32)
        out_specs = pl.BlockSpec((tm, D_MODEL), row)
    else:
        out_shape = (jax.ShapeDtypeStruct((m, D_MODEL), F32),
                     jax.ShapeDtypeStruct((m, D_MODEL), BF16))
        out_specs = (pl.BlockSpec((tm, D_MODEL), row), pl.BlockSpec((tm, D_MODEL), row))
    return pl.pallas_call(
        functools.partial(_post_kernel, emit_x=not last),
        out_shape=out_shape,
        grid=(m // tm,),
        in_specs=[pl.BlockSpec((tm, MIX_W), row),
                  pl.BlockSpec((tm, MIX_W), row),
                  pl.BlockSpec((tm, GATE_W), row),
                  pl.BlockSpec((tm, D_MODEL), row),
                  _resident_spec((MIX_W, D_MODEL), layer),
                  _resident_spec((MIX_W, D_MODEL), layer),
                  _resident_spec((D_MODEL, D_MODEL), layer),
                  _resident_spec((D_MODEL, FFN_HIDDEN), layer),
                  _resident_spec((D_MODEL, FFN_HIDDEN), layer),
                  _resident_spec((FFN_HIDDEN, D_MODEL), layer),
                  _layer_spec((1, D_MODEL), layer),
                  _layer_spec((1, D_MODEL), next_layer)],
        out_specs=out_specs,
        scratch_shapes=[pltpu.VMEM((tm, FFN_HIDDEN), BF16)],
        compiler_params=_cparams(("parallel",)),
        name="post",
    )(attn_o, hg_o, gates, x, wa, wb, wo, wg, wu, wd, ffn_gains, next_gains)


def _rope_expand():
    half = ROT_DIM // 2
    e = np.zeros((2 * half, 3 * HEAD_W), np.float32)
    base = np.zeros((1, 3 * HEAD_W), np.float32)
    for l in range(HEAD_W):
        d = l % QK_DIM
        if d < half:
            e[d, l] = 1.0
            e[half + d, HEAD_W + l] = -1.0
        elif d < ROT_DIM:
            e[d - half, l] = 1.0
            e[half + d - half, 2 * HEAD_W + l] = 1.0
        else:
            base[0, l] = 1.0
    return e, base


def _rope_tables(positions):
    inv_freq = ROPE_THETA ** (-(jnp.arange(0, ROT_DIM, 2, dtype=F32) / ROT_DIM))
    ang = positions.astype(F32).reshape(-1, 1) * inv_freq
    cs = jnp.concatenate([jnp.cos(ang), jnp.sin(ang)], axis=1)
    e, base = _rope_expand()
    return jnp.dot(cs, jnp.asarray(e), precision=lax.Precision.HIGHEST) + jnp.asarray(base)


def kernel(x, positions, w_in, da_lambda, da_norm, hg_lb_logits, hg_norm, w_a, w_b, w_o,
           attn_norm, ffn_norm, w_gate, w_up, w_down, final_norm):
    batch, seq, _ = x.shape
    depth = w_in.shape[0]
    m = batch * seq
    tm = min(512, seq)
    tq = tk = tm

    rope_t = _rope_tables(positions)
    p = jax.nn.softmax(hg_lb_logits.astype(F32), axis=1)
    csum = jnp.cumsum(p, axis=1)
    lbs = csum - csum[:, :1]

    bf = lambda w: w.astype(BF16)
    w_in, w_a, w_b, w_o, w_gate, w_up, w_down = map(bf, (w_in, w_a, w_b, w_o, w_gate, w_up, w_down))
    attn_gains = attn_norm.astype(F32).reshape(depth, 1, D_MODEL)
    ffn_gains = ffn_norm.astype(F32).reshape(depth, 1, D_MODEL)
    final_gain = final_norm.astype(F32).reshape(1, 1, D_MODEL)
    da_gains = da_norm.astype(F32).reshape(depth, N_HEADS, 1, HEAD_W)
    hg_gains = hg_norm.astype(F32).reshape(depth, N_HEADS, 1, HEAD_W)

    xf = x.reshape(m, D_MODEL).astype(F32)
    h = None
    out = None
    for layer in range(depth):
        lam_init = 0.8 - 0.6 * math.exp(-0.3 * layer)
        l32 = da_lambda[layer].astype(F32)
        lam_full = (jnp.exp(jnp.sum(l32[0] * l32[1])) - jnp.exp(jnp.sum(l32[2] * l32[3]))
                    + lam_init)
        lam = lam_full.reshape(1).astype(F32)
        lb_layer = lbs[:, layer].reshape(2, N_HEADS, HEAD_W).transpose(1, 0, 2)

        if layer == 0:
            qk, vt, hg, gates = _in_proj(xf, w_in, layer, rope_t, tm, gains=attn_gains)
        else:
            qk, vt, hg, gates = _in_proj(h, w_in, layer, rope_t, tm)
        attn_o, hg_o = _mixers(qk, vt, hg, lam, lb_layer, da_gains, hg_gains, layer, batch, seq,
                               tq, tk, min(512, seq), 1.0 - lam_init)
        last = layer == depth - 1
        args = (attn_o.reshape(m, MIX_W), hg_o.reshape(m, MIX_W), gates, xf,
                w_a, w_b, w_o, w_gate, w_up, w_down, layer, ffn_gains)
        if last:
            out = _post(*args, final_gain, 0, tm, True)
        else:
            xf, h = _post(*args, attn_gains, layer + 1, tm, False)
    return out.reshape(batch, seq, D_MODEL).astype(x.dtype)
```

```python
import functools
import math

import numpy as np
import jax
import jax.numpy as jnp
from jax import lax
from jax.experimental import pallas as pl
from jax.experimental.pallas import tpu as pltpu

F32 = jnp.float32
BF16 = jnp.bfloat16

D_MODEL = 1024
N_HEADS = 4
HEAD_W = 128
QK_DIM = 64
ROT_DIM = 16
ROPE_THETA = 500000.0
FFN_HIDDEN = 2816
NORM_EPS = 1e-6
MIX_W = N_HEADS * HEAD_W
HG_W = 5 * MIX_W
GATE_W = 2 * D_MODEL
IN_WIDTH = 3 * MIX_W + HG_W + GATE_W

LANES = 128
SUBLANES = 8
MXU_N = 256
VMEM_LIMIT = 56 * 1024 * 1024

HG_CHUNK = 128
HG_LEVELS = HG_CHUNK.bit_length() - 1
HG_LOW_LEVELS = SUBLANES.bit_length() - 1
HG_TILES = HG_CHUNK // SUBLANES

NT_DIMS = (((1,), (1,)), ((), ()))
Q_SCALE = QK_DIM ** -0.5 * math.log2(math.e)
ONES_ROWS = 16


def _cparams(sem):
    return pltpu.CompilerParams(dimension_semantics=sem, vmem_limit_bytes=VMEM_LIMIT)


def _sigmoid(x):
    return 1.0 / (1.0 + jnp.exp(-x))


def _rms_scale(x, gain):
    ms = jnp.mean(x * x, axis=-1, keepdims=True)
    return x * lax.rsqrt(ms + NORM_EPS) * gain


def _layer_spec(shape, layer):
    zeros = (0,) * len(shape)
    return pl.BlockSpec((None,) + tuple(shape), lambda *_: (layer,) + zeros)


def _resident_spec(shape, layer):
    zeros = (0,) * len(shape)
    return pl.BlockSpec((None,) + tuple(shape), lambda *_: (layer,) + zeros,
                        pipeline_mode=pl.Buffered(1))


def _proj_kernel(h_ref, w_ref, rope_ref, *rest, norm_input):
    if norm_input:
        g_ref, qk_ref, vt_ref, hg_ref, gate_ref = rest
        h = _rms_scale(h_ref[...], g_ref[...]).astype(BF16)
    else:
        qk_ref, vt_ref, hg_ref, gate_ref = rest
        h = h_ref[...]
    cos_t = rope_ref[:, 0:HEAD_W]
    sin_a = rope_ref[:, HEAD_W:2 * HEAD_W]
    sin_b = rope_ref[:, 2 * HEAD_W:3 * HEAD_W]

    def rope(a, scale):
        outs = []
        for hh in range(N_HEADS):
            t = a[:, hh * HEAD_W:(hh + 1) * HEAD_W]
            r = (t * cos_t + pltpu.roll(t, HEAD_W - ROT_DIM // 2, 1) * sin_a
                 + pltpu.roll(t, ROT_DIM // 2, 1) * sin_b)
            outs.append(r * scale if scale != 1.0 else r)
        return jnp.concatenate(outs, axis=1)

    n_groups = IN_WIDTH // MIX_W
    for g in range(n_groups):
        acc = jnp.dot(h, w_ref[:, g * MIX_W:(g + 1) * MIX_W].astype(BF16),
                      preferred_element_type=F32)
        if g == 0:
            qk_ref[:, 0:MIX_W] = rope(acc, Q_SCALE).astype(BF16)
        elif g == 1:
            qk_ref[:, MIX_W:2 * MIX_W] = rope(acc, 1.0).astype(BF16)
        elif g == 2:
            vt_ref[0] = acc.T.astype(BF16)
        elif g < 8:
            hg_ref[:, (g - 3) * MIX_W:(g - 2) * MIX_W] = acc
        else:
            gate_ref[:, (g - 8) * MIX_W:(g - 7) * MIX_W] = _sigmoid(acc).astype(BF16)


def _in_proj(h, w_all, layer, rope_t, tm, gains=None):
    m = h.shape[0]
    row = lambda i: (i, 0)
    extra_specs = [] if gains is None else [_layer_spec((1, D_MODEL), layer)]
    extra_args = [] if gains is None else [gains]
    return pl.pallas_call(
        functools.partial(_proj_kernel, norm_input=gains is not None),
        out_shape=(jax.ShapeDtypeStruct((m, 2 * MIX_W), BF16),
                   jax.ShapeDtypeStruct((m // tm, MIX_W, tm), BF16),
                   jax.ShapeDtypeStruct((m, HG_W), F32),
                   jax.ShapeDtypeStruct((m, GATE_W), BF16)),
        grid=(m // tm,),
        in_specs=[pl.BlockSpec((tm, D_MODEL), row),
                  _resident_spec((D_MODEL, IN_WIDTH), layer),
                  pl.BlockSpec((tm, 3 * HEAD_W), row)] + extra_specs,
        out_specs=(pl.BlockSpec((tm, 2 * MIX_W), row),
                   pl.BlockSpec((1, MIX_W, tm), lambda i: (i, 0, 0)),
                   pl.BlockSpec((tm, HG_W), row),
                   pl.BlockSpec((tm, GATE_W), row)),
        compiler_params=_cparams(("parallel",)),
        name="in_proj",
    )(h, w_all, rope_t, *extra_args)


def _attn_body(lam_ref, q_ref, k_ref, vt_ref, g_ref, o_ref, s_scr, acc_scr, fin_scr,
               *, tq, tk, out_scale, side_work):
    seq = k_ref.shape[1]
    nk = seq // tk
    nq = seq // tq
    lane = lax.broadcasted_iota(jnp.int32, (tq, HEAD_W), 1)
    ones = jnp.ones((ONES_ROWS, tk), BF16)
    m0 = jnp.full((1, 2 * tq), -jnp.inf, F32)
    lam = lam_ref[0]
    gain = g_ref[0]

    def logits(u, j, slot):
        q = q_ref[0, pl.ds(pl.multiple_of(u * tq, tq), tq), :]
        zero = jnp.zeros_like(q)
        qs = jnp.concatenate([jnp.where(lane < QK_DIM, q, zero),
                              jnp.where(lane >= QK_DIM, q, zero)], axis=0)
        st = lax.dot_general(k_ref[0, pl.ds(pl.multiple_of(j * tk, tk), tk), :], qs, NT_DIMS,
                             preferred_element_type=F32)
        s_scr[slot] = st
        return jnp.max(st, axis=0, keepdims=True)

    def accumulate(j, slot, m_old, mx):
        m_new = jnp.maximum(m_old, mx)
        alpha = jnp.exp2(m_old - m_new)
        p = jnp.exp2(s_scr[slot] - m_new).astype(BF16)
        vt_ext = jnp.concatenate([vt_ref[j], ones], axis=0)
        acc_scr[...] = acc_scr[...] * alpha + jnp.dot(vt_ext, p, preferred_element_type=F32)
        return m_new

    def pair(jj, carry, u):
        m, mx = carry
        j = 2 * jj
        mx1 = logits(u, j + 1, 1)
        m = accumulate(j, 0, m, mx)
        mx2 = logits(u, j + 2, 0)
        m = accumulate(j + 1, 1, m, mx1)
        return m, mx2

    def finalize(u):
        acc = fin_scr[...]
        o_t = acc[:HEAD_W, :] / acc[HEAD_W:HEAD_W + 1, :]
        d_t = o_t[:, :tq] - lam * o_t[:, tq:]
        o_ref[0, pl.ds(pl.multiple_of(u * tq, tq), tq), :] = (
            _rms_scale(d_t.T, gain) * out_scale).astype(o_ref.dtype)

    def tile(u, mx):
        finalize(jnp.maximum(u - 1, 0))
        side_work(u)
        m, mx = lax.fori_loop(0, nk // 2 - 1, functools.partial(pair, u=u), (m0, mx), unroll=True)
        mx1 = logits(u, nk - 1, 1)
        m = accumulate(nk - 2, 0, m, mx)
        mx_next = logits(jnp.minimum(u + 1, nq - 1), 0, 0)
        accumulate(nk - 1, 1, m, mx1)
        fin_scr[...] = acc_scr[...]
        return mx_next

    acc_scr[...] = jnp.zeros(acc_scr.shape, F32)
    fin_scr[...] = jnp.ones(fin_scr.shape, F32)
    lax.fori_loop(0, nq, tile, logits(0, 0, 0))
    finalize(nq - 1)


def _hg_masks():
    c = HG_CHUNK
    t = np.arange(c)[:, None]
    s = np.arange(c)[None, :]
    full = []
    for lvl in range(HG_LEVELS):
        h = 1 << lvl
        same = (t // (2 * h)) == (s // (2 * h))
        full.append(same & ((t & h) != 0) & ((s & h) == 0))
    low = ([full[l] for l in range(HG_LOW_LEVELS)] + [full[l].T for l in range(HG_LOW_LEVELS)]
           + [t == s])
    high = []
    for rev in (False, True):
        for lvl in range(HG_LOW_LEVELS, HG_LEVELS):
            h = 1 << lvl
            rows = np.arange(c)
            qrows = rows[(rows & h) == 0] if rev else rows[(rows & h) != 0]
            mk = full[lvl].T if rev else full[lvl]
            high.append(mk[qrows])
    return np.stack(low).astype(np.float32), np.stack(high).astype(np.float32)


def _hg_chunk(q, z, v, st_t, lb, mlow_ref, mhigh_ref, rev):
    c = HG_CHUNK
    oml = 1.0 - lb
    ez = jnp.exp(-jnp.abs(z))
    r = 1.0 / (1.0 + ez)
    er = ez * r
    pos = z >= 0.0
    f = lb + oml * jnp.where(pos, r, er)
    k = oml * jnp.where(pos, er, r)
    row = lax.broadcasted_iota(jnp.int32, (c, HEAD_W), 0)
    a_in = f
    e_ex = jnp.ones_like(f)
    tot = f
    sc = jnp.sum(q * k, axis=1, keepdims=True) * mlow_ref[2 * HG_LOW_LEVELS]
    for lvl in range(HG_LOW_LEVELS):
        h = 1 << lvl
        bit = (row & h) != 0
        qside = jnp.logical_not(bit) if rev else bit
        x = jnp.where(qside, q * a_in, k * e_ex)
        lv = jnp.dot(x.astype(BF16), x.T.astype(BF16), preferred_element_type=F32)
        sc = sc + lv * mlow_ref[(HG_LOW_LEVELS if rev else 0) + lvl]
        tot3 = tot.reshape(HG_TILES, SUBLANES, HEAD_W)
        prev = pltpu.roll(tot3, h, 1).reshape(c, HEAD_W)
        nxt = pltpu.roll(tot3, SUBLANES - h, 1).reshape(c, HEAD_W)
        if rev:
            a_in = a_in * jnp.where(bit, 1.0, nxt)
            e_ex = e_ex * jnp.where(bit, prev, 1.0)
        else:
            a_in = a_in * jnp.where(bit, prev, 1.0)
            e_ex = e_ex * jnp.where(bit, 1.0, nxt)
        tot = tot * jnp.where(bit, prev, nxt)

    tiles = lambda arr: [arr[i * SUBLANES:(i + 1) * SUBLANES] for i in range(HG_TILES)]
    q_t, k_t, a_t, e_t, sc_t = tiles(q), tiles(k), tiles(a_in), tiles(e_ex), tiles(sc)
    tb = tiles(tot)
    for lvl in range(HG_LOW_LEVELS, HG_LEVELS):
        ht = (1 << lvl) // SUBLANES
        n_blocks = HG_TILES // (2 * ht)
        x_t = [None] * HG_TILES
        q_idx = []
        for b in range(n_blocks):
            lo = range(2 * b * ht, (2 * b + 1) * ht)
            hi = range((2 * b + 1) * ht, (2 * b + 2) * ht)
            q_half, k_half = (lo, hi) if rev else (hi, lo)
            for i in q_half:
                x_t[i] = q_t[i] * a_t[i]
            for i in k_half:
                x_t[i] = k_t[i] * e_t[i]
            q_idx.extend(q_half)
        x = jnp.concatenate(x_t, axis=0)
        xq = jnp.concatenate([x_t[i] for i in q_idx], axis=0).astype(BF16)
        lv = jnp.dot(xq, x.T.astype(BF16), preferred_element_type=F32)
        mk = mhigh_ref[(HG_LEVELS - HG_LOW_LEVELS if rev else 0) + lvl - HG_LOW_LEVELS]
        for n, i in enumerate(q_idx):
            rows = slice(n * SUBLANES, (n + 1) * SUBLANES)
            sc_t[i] = sc_t[i] + lv[rows] * mk[rows]
        new_tb = []
        for b in range(n_blocks):
            t_lo, t_hi = tb[2 * b], tb[2 * b + 1]
            lo = range(2 * b * ht, (2 * b + 1) * ht)
            hi = range((2 * b + 1) * ht, (2 * b + 2) * ht)
            if rev:
                for i in lo:
                    a_t[i] = a_t[i] * t_hi
                for i in hi:
                    e_t[i] = e_t[i] * t_lo
            else:
                for i in hi:
                    a_t[i] = a_t[i] * t_lo
                for i in lo:
                    e_t[i] = e_t[i] * t_hi
            new_tb.append(t_lo * t_hi)
        tb = new_tb

    sc = jnp.concatenate(sc_t, axis=0)
    qa = jnp.concatenate([q_t[i] * a_t[i] for i in range(HG_TILES)], axis=0)
    ke = jnp.concatenate([k_t[i] * e_t[i] for i in range(HG_TILES)], axis=0)
    vb = v.astype(BF16)
    intra = jnp.dot(sc.astype(BF16), vb, preferred_element_type=F32)
    inter = lax.dot_general(qa.astype(BF16), st_t.astype(BF16), NT_DIMS,
                            preferred_element_type=F32)
    upd = jnp.dot(v.T.astype(BF16), ke.astype(BF16), preferred_element_type=F32)
    st_t = st_t * tb[0][0:1, :] + upd
    return inter + intra, st_t


def _mixer_kernel(lam_ref, q_ref, k_ref, vt_ref, ag_ref, hq_ref, zf_ref, zb_ref, hv_ref, hg_ref,
                  lb_ref, hgain_ref, mlow_ref, mhigh_ref, ao_ref, ho_ref,
                  s_scr, acc_scr, fin_scr, of_scr, ob_scr, st_scr, *, tq, tk, tr, out_scale):
    seq = q_ref.shape[1]
    c = HG_CHUNK
    n = seq // c
    per_tile = n // (seq // tq)
    lb_f = lb_ref[0, 0:1, :]
    lb_b = lb_ref[0, 1:2, :]
    st_scr[...] = jnp.zeros(st_scr.shape, F32)

    def scans(u):
        st_f = st_scr[0]
        st_b = st_scr[1]
        for r in range(per_tile):
            i = u * per_tile + r
            cf = pl.multiple_of(i * c, c)
            cb = pl.multiple_of((n - 1 - i) * c, c)
            o_f, st_f = _hg_chunk(hq_ref[0, pl.ds(cf, c), :], zf_ref[0, pl.ds(cf, c), :],
                                  hv_ref[0, pl.ds(cf, c), :], st_f, lb_f, mlow_ref, mhigh_ref,
                                  False)
            of_scr[pl.ds(cf, c), :] = o_f
            o_b, st_b = _hg_chunk(hq_ref[0, pl.ds(cb, c), :], zb_ref[0, pl.ds(cb, c), :],
                                  hv_ref[0, pl.ds(cb, c), :], st_b, lb_b, mlow_ref, mhigh_ref,
                                  True)
            ob_scr[pl.ds(cb, c), :] = o_b
        st_scr[0] = st_f
        st_scr[1] = st_b

    _attn_body(lam_ref, q_ref, k_ref, vt_ref, ag_ref, ao_ref, s_scr, acc_scr, fin_scr,
               tq=tq, tk=tk, out_scale=out_scale, side_work=scans)

    gain = hgain_ref[0]

    def finish(i, carry):
        off = pl.multiple_of(i * tr, tr)
        o = of_scr[pl.ds(off, tr), :] + ob_scr[pl.ds(off, tr), :]
        y = _rms_scale(o, gain) * _sigmoid(hg_ref[0, pl.ds(off, tr), :])
        ho_ref[0, pl.ds(off, tr), :] = y.astype(ho_ref.dtype)
        return carry

    lax.fori_loop(0, seq // tr, finish, 0)


def _mixers(qk, vt, hg, lam, lbs, da_gains, hg_gains, layer, batch, seq, tq, tk, tr, out_scale):
    qk3 = qk.reshape(batch, seq, 2 * MIX_W)
    hg3 = hg.reshape(batch, seq, HG_W)
    nk = seq // tk
    assert nk % 2 == 0, "the attention pipeline walks key chunks in pairs"
    assert (seq // HG_CHUNK) % (seq // tq) == 0, "whole scan chunks per query tile"
    mlow, mhigh = (jnp.asarray(a) for a in _hg_masks())
    col = lambda g: (lambda b, h: (b, 0, g * N_HEADS + h))
    head_gain = pl.BlockSpec((None, 1, 1, HEAD_W), lambda b, h: (layer, h, 0, 0))
    seq_block = lambda g: pl.BlockSpec((1, seq, HEAD_W), col(g))
    out = jax.ShapeDtypeStruct((batch, seq, MIX_W), BF16)
    kern = functools.partial(_mixer_kernel, tq=tq, tk=tk, tr=tr, out_scale=out_scale)
    return pl.pallas_call(
        kern,
        out_shape=(out, out),
        grid=(batch, N_HEADS),
        in_specs=[pl.BlockSpec(memory_space=pltpu.SMEM),
                  seq_block(0),
                  seq_block(1),
                  pl.BlockSpec((nk, HEAD_W, tk), lambda b, h: (b, h, 0)),
                  head_gain,
                  seq_block(0), seq_block(1), seq_block(2), seq_block(3), seq_block(4),
                  pl.BlockSpec((1, 2, HEAD_W), lambda b, h: (h, 0, 0)),
                  head_gain,
                  pl.BlockSpec(mlow.shape, lambda b, h: (0, 0, 0)),
                  pl.BlockSpec(mhigh.shape, lambda b, h: (0, 0, 0))],
        out_specs=(pl.BlockSpec((1, seq, HEAD_W), lambda b, h: (b, 0, h)),
                   pl.BlockSpec((1, seq, HEAD_W), lambda b, h: (b, 0, h))),
        scratch_shapes=[pltpu.VMEM((2, tk, 2 * tq), F32),
                        pltpu.VMEM((HEAD_W + ONES_ROWS, 2 * tq), F32),
                        pltpu.VMEM((HEAD_W + ONES_ROWS, 2 * tq), F32),
                        pltpu.VMEM((seq, HEAD_W), F32),
                        pltpu.VMEM((seq, HEAD_W), F32),
                        pltpu.VMEM((2, HEAD_W, HEAD_W), F32)],
        compiler_params=_cparams(("parallel", "parallel")),
        name="mixers",
    )(lam, qk3, qk3, vt, da_gains, hg3, hg3, hg3, hg3, hg3, lbs, hg_gains, mlow, mhigh)


def _post_kernel(a_ref, b_ref, gate_ref, x_ref, wa_ref, wb_ref, wo_ref, wg_ref, wu_ref, wd_ref,
                 g1_ref, g2_ref, *out_refs, emit_x):
    act_scr = out_refs[-1]
    ya = jnp.dot(a_ref[...], wa_ref[...].astype(BF16), preferred_element_type=F32)
    yb = jnp.dot(b_ref[...], wb_ref[...].astype(BF16), preferred_element_type=F32)
    merged = (gate_ref[:, :D_MODEL].astype(F32) * ya
              + gate_ref[:, D_MODEL:].astype(F32) * yb).astype(BF16)
    x_mid = x_ref[...] + jnp.dot(merged, wo_ref[...].astype(BF16), preferred_element_type=F32)
    h = _rms_scale(x_mid, g1_ref[...]).astype(BF16)
    for c in range(FFN_HIDDEN // MXU_N):
        sl = slice(c * MXU_N, (c + 1) * MXU_N)
        gate = jnp.dot(h, wg_ref[:, sl], preferred_element_type=F32)
        up = jnp.dot(h, wu_ref[:, sl], preferred_element_type=F32)
        act_scr[:, sl] = (gate * _sigmoid(gate) * up).astype(BF16)
    x_new = x_mid + jnp.dot(act_scr[...], wd_ref[...], preferred_element_type=F32)
    normed = _rms_scale(x_new, g2_ref[...])
    if emit_x:
        out_refs[0][...] = x_new
        out_refs[1][...] = normed.astype(out_refs[1].dtype)
    else:
        out_refs[0][...] = normed.astype(out_refs[0].dtype)


def _post(attn_o, hg_o, gates, x, wa, wb, wo, wg, wu, wd, layer, ffn_gains, next_gains,
          next_layer, tm, last):
    m = x.shape[0]
    row = lambda i: (i, 0)
    if last:
        out_shape = jax.ShapeDtypeStruct((m, D_MODEL), F32)
        out_specs = pl.BlockSpec((tm, D_MODEL), row)
    else:
        out_shape = (jax.ShapeDtypeStruct((m, D_MODEL), F32),
                     jax.ShapeDtypeStruct((m, D_MODEL), BF16))
        out_specs = (pl.BlockSpec((tm, D_MODEL), row), pl.BlockSpec((tm, D_MODEL), row))
    return pl.pallas_call(
        functools.partial(_post_kernel, emit_x=not last),
        out_shape=out_shape,
        grid=(m // tm,),
        in_specs=[pl.BlockSpec((tm, MIX_W), row),
                  pl.BlockSpec((tm, MIX_W), row),
                  pl.BlockSpec((tm, GATE_W), row),
                  pl.BlockSpec((tm, D_MODEL), row),
                  _resident_spec((MIX_W, D_MODEL), layer),
                  _resident_spec((MIX_W, D_MODEL), layer),
                  _resident_spec((D_MODEL, D_MODEL), layer),
                  _resident_spec((D_MODEL, FFN_HIDDEN), layer),
                  _resident_spec((D_MODEL, FFN_HIDDEN), layer),
                  _resident_spec((FFN_HIDDEN, D_MODEL), layer),
                  _layer_spec((1, D_MODEL), layer),
                  _layer_spec((1, D_MODEL), next_layer)],
        out_specs=out_specs,
        scratch_shapes=[pltpu.VMEM((tm, FFN_HIDDEN), BF16)],
        compiler_params=_cparams(("parallel",)),
        name="post",
    )(attn_o, hg_o, gates, x, wa, wb, wo, wg, wu, wd, ffn_gains, next_gains)


def _rope_expand():
    half = ROT_DIM // 2
    e = np.zeros((2 * half, 3 * HEAD_W), np.float32)
    base = np.zeros((1, 3 * HEAD_W), np.float32)
    for l in range(HEAD_W):
        d = l % QK_DIM
        if d < half:
            e[d, l] = 1.0
            e[half + d, HEAD_W + l] = -1.0
        elif d < ROT_DIM:
            e[d - half, l] = 1.0
            e[half + d - half, 2 * HEAD_W + l] = 1.0
        else:
            base[0, l] = 1.0
    return e, base


def _rope_tables(positions):
    inv_freq = ROPE_THETA ** (-(jnp.arange(0, ROT_DIM, 2, dtype=F32) / ROT_DIM))
    ang = positions.astype(F32).reshape(-1, 1) * inv_freq
    cs = jnp.concatenate([jnp.cos(ang), jnp.sin(ang)], axis=1)
    e, base = _rope_expand()
    return jnp.dot(cs, jnp.asarray(e), precision=lax.Precision.HIGHEST) + jnp.asarray(base)


def kernel(x, positions, w_in, da_lambda, da_norm, hg_lb_logits, hg_norm, w_a, w_b, w_o,
           attn_norm, ffn_norm, w_gate, w_up, w_down, final_norm):
    batch, seq, _ = x.shape
    depth = w_in.shape[0]
    m = batch * seq
    tm = min(512, seq)
    tq = tk = tm

    rope_t = _rope_tables(positions)
    p = jax.nn.softmax(hg_lb_logits.astype(F32), axis=1)
    csum = jnp.cumsum(p, axis=1)
    lbs = csum - csum[:, :1]

    w_gate, w_up, w_down = (w.astype(BF16) for w in (w_gate, w_up, w_down))
    w_in, w_a, w_b, w_o = (w.astype(F32) for w in (w_in, w_a, w_b, w_o))
    attn_gains = attn_norm.astype(F32).reshape(depth, 1, D_MODEL)
    ffn_gains = ffn_norm.astype(F32).reshape(depth, 1, D_MODEL)
    final_gain = final_norm.astype(F32).reshape(1, 1, D_MODEL)
    da_gains = da_norm.astype(F32).reshape(depth, N_HEADS, 1, HEAD_W)
    hg_gains = hg_norm.astype(F32).reshape(depth, N_HEADS, 1, HEAD_W)

    xf = x.reshape(m, D_MODEL).astype(F32)
    h = None
    out = None
    for layer in range(depth):
        lam_init = 0.8 - 0.6 * math.exp(-0.3 * layer)
        l32 = da_lambda[layer].astype(F32)
        lam_full = (jnp.exp(jnp.sum(l32[0] * l32[1])) - jnp.exp(jnp.sum(l32[2] * l32[3]))
                    + lam_init)
        lam = lam_full.reshape(1).astype(F32)
        lb_layer = lbs[:, layer].reshape(2, N_HEADS, HEAD_W).transpose(1, 0, 2)

        if layer == 0:
            qk, vt, hg, gates = _in_proj(xf, w_in, layer, rope_t, tm, gains=attn_gains)
        else:
            qk, vt, hg, gates = _in_proj(h, w_in, layer, rope_t, tm)
        attn_o, hg_o = _mixers(qk, vt, hg, lam, lb_layer, da_gains, hg_gains, layer, batch, seq,
                               tq, tk, min(512, seq), 1.0 - lam_init)
        last = layer == depth - 1
        args = (attn_o.reshape(m, MIX_W), hg_o.reshape(m, MIX_W), gates, xf,
                w_a, w_b, w_o, w_gate, w_up, w_down, layer, ffn_gains)
        if last:
            out = _post(*args, final_gain, 0, tm, True)
        else:
            xf, h = _post(*args, attn_gains, layer + 1, tm, False)
    return out.reshape(batch, seq, D_MODEL).astype(x.dtype)
```

```python
import functools
import math

import numpy as np
import jax
import jax.numpy as jnp
from jax import lax
from jax.experimental import pallas as pl
from jax.experimental.pallas import tpu as pltpu

F32 = jnp.float32
BF16 = jnp.bfloat16

D_MODEL = 1024
N_HEADS = 4
HEAD_W = 128
QK_DIM = 64
ROT_DIM = 16
ROPE_THETA = 500000.0
FFN_HIDDEN = 2816
NORM_EPS = 1e-6
MIX_W = N_HEADS * HEAD_W
HG_W = 5 * MIX_W
GATE_W = 2 * D_MODEL
IN_WIDTH = 3 * MIX_W + HG_W + GATE_W

LANES = 128
SUBLANES = 8
MXU_N = 256
VMEM_LIMIT = 56 * 1024 * 1024

HG_CHUNK = 128
HG_LEVELS = HG_CHUNK.bit_length() - 1
HG_LOW_LEVELS = SUBLANES.bit_length() - 1
HG_TILES = HG_CHUNK // SUBLANES

Q_SCALE = QK_DIM ** -0.5 * math.log2(math.e)
ONES_ROWS = 16


def _cparams(sem):
    return pltpu.CompilerParams(dimension_semantics=sem, vmem_limit_bytes=VMEM_LIMIT)


def _sigmoid(x):
    return 1.0 / (1.0 + jnp.exp(-x))


def _rms_scale(x, gain):
    ms = jnp.mean(x * x, axis=-1, keepdims=True)
    return x * lax.rsqrt(ms + NORM_EPS) * gain


def _layer_spec(shape, layer):
    zeros = (0,) * len(shape)
    return pl.BlockSpec((None,) + tuple(shape), lambda *_: (layer,) + zeros)


def _resident_spec(shape, layer):
    zeros = (0,) * len(shape)
    return pl.BlockSpec((None,) + tuple(shape), lambda *_: (layer,) + zeros,
                        pipeline_mode=pl.Buffered(1))


def _proj_kernel(h_ref, w_ref, rope_ref, *rest, norm_input):
    if norm_input:
        g_ref, qk_ref, vt_ref, hg_ref, gate_ref = rest
        h = _rms_scale(h_ref[...], g_ref[...]).astype(BF16)
    else:
        qk_ref, vt_ref, hg_ref, gate_ref = rest
        h = h_ref[...]
    cos_t = rope_ref[:, 0:HEAD_W]
    sin_a = rope_ref[:, HEAD_W:2 * HEAD_W]
    sin_b = rope_ref[:, 2 * HEAD_W:3 * HEAD_W]

    def rope(a, scale):
        outs = []
        for hh in range(N_HEADS):
            t = a[:, hh * HEAD_W:(hh + 1) * HEAD_W]
            r = (t * cos_t + pltpu.roll(t, HEAD_W - ROT_DIM // 2, 1) * sin_a
                 + pltpu.roll(t, ROT_DIM // 2, 1) * sin_b)
            outs.append(r * scale if scale != 1.0 else r)
        return jnp.concatenate(outs, axis=1)

    n_groups = IN_WIDTH // MIX_W
    for g in range(n_groups):
        acc = jnp.dot(h, w_ref[:, g * MIX_W:(g + 1) * MIX_W].astype(BF16),
                      preferred_element_type=F32)
        if g == 0:
            qk_ref[:, 0:MIX_W] = rope(acc, Q_SCALE).astype(BF16)
        elif g == 1:
            qk_ref[:, MIX_W:2 * MIX_W] = rope(acc, 1.0).astype(BF16)
        elif g == 2:
            vt_ref[0] = acc.T.astype(BF16)
        elif g < 8:
            hg_ref[:, (g - 3) * MIX_W:(g - 2) * MIX_W] = acc
        else:
            gate_ref[:, (g - 8) * MIX_W:(g - 7) * MIX_W] = _sigmoid(acc).astype(BF16)


def _in_proj(h, w_all, layer, rope_t, tm, gains=None):
    m = h.shape[0]
    row = lambda i: (i, 0)
    extra_specs = [] if gains is None else [_layer_spec((1, D_MODEL), layer)]
    extra_args = [] if gains is None else [gains]
    return pl.pallas_call(
        functools.partial(_proj_kernel, norm_input=gains is not None),
        out_shape=(jax.ShapeDtypeStruct((m, 2 * MIX_W), BF16),
                   jax.ShapeDtypeStruct((m // tm, MIX_W, tm), BF16),
                   jax.ShapeDtypeStruct((m, HG_W), F32),
                   jax.ShapeDtypeStruct((m, GATE_W), BF16)),
        grid=(m // tm,),
        in_specs=[pl.BlockSpec((tm, D_MODEL), row),
                  _resident_spec((D_MODEL, IN_WIDTH), layer),
                  pl.BlockSpec((tm, 3 * HEAD_W), row)] + extra_specs,
        out_specs=(pl.BlockSpec((tm, 2 * MIX_W), row),
                   pl.BlockSpec((1, MIX_W, tm), lambda i: (i, 0, 0)),
                   pl.BlockSpec((tm, HG_W), row),
                   pl.BlockSpec((tm, GATE_W), row)),
        compiler_params=_cparams(("parallel",)),
        name="in_proj",
    )(h, w_all, rope_t, *extra_args)


def _attn_body(lam_ref, q_ref, k_ref, vt_ref, g_ref, o_ref, s_scr, acc_scr, fin_scr, qt_scr,
               *, tq, tk, out_scale, side_work):
    seq = k_ref.shape[1]
    nk = seq // tk
    nq = seq // tq
    lane = lax.broadcasted_iota(jnp.int32, (tq, HEAD_W), 1)
    ones = jnp.ones((ONES_ROWS, tk), BF16)
    m0 = jnp.full((1, 2 * tq), -jnp.inf, F32)
    lam = lam_ref[0]
    gain = g_ref[0]

    def load_queries(u):
        q = q_ref[0, pl.ds(pl.multiple_of(u * tq, tq), tq), :].astype(F32)
        qs = jnp.concatenate([jnp.where(lane < QK_DIM, q, 0.0),
                              jnp.where(lane >= QK_DIM, q, 0.0)], axis=0)
        qt_scr[...] = qs.T.astype(BF16)

    def logits(j, slot):
        st = jnp.dot(k_ref[0, pl.ds(pl.multiple_of(j * tk, tk), tk), :], qt_scr[...],
                     preferred_element_type=F32)
        s_scr[slot] = st
        return jnp.max(st, axis=0, keepdims=True)

    def accumulate(j, slot, m_old, mx):
        m_new = jnp.maximum(m_old, mx)
        alpha = jnp.exp2(m_old - m_new)
        p = jnp.exp2(s_scr[slot] - m_new).astype(BF16)
        vt_ext = jnp.concatenate([vt_ref[j], ones], axis=0)
        acc_scr[...] = acc_scr[...] * alpha + jnp.dot(vt_ext, p, preferred_element_type=F32)
        return m_new

    def pair(jj, carry, u):
        m, mx = carry
        j = 2 * jj
        mx1 = logits(j + 1, 1)
        m = accumulate(j, 0, m, mx)
        mx2 = logits(j + 2, 0)
        m = accumulate(j + 1, 1, m, mx1)
        return m, mx2

    def finalize(u):
        acc = fin_scr[...]
        o_t = acc[:HEAD_W, :] / acc[HEAD_W:HEAD_W + 1, :]
        d_t = o_t[:, :tq] - lam * o_t[:, tq:]
        o_ref[0, pl.ds(pl.multiple_of(u * tq, tq), tq), :] = (
            _rms_scale(d_t.T, gain) * out_scale).astype(o_ref.dtype)

    def tile(u, mx):
        finalize(jnp.maximum(u - 1, 0))
        side_work(u)
        m, mx = lax.fori_loop(0, nk // 2 - 1, functools.partial(pair, u=u), (m0, mx), unroll=True)
        mx1 = logits(nk - 1, 1)
        m = accumulate(nk - 2, 0, m, mx)
        load_queries(jnp.minimum(u + 1, nq - 1))
        mx_next = logits(0, 0)
        accumulate(nk - 1, 1, m, mx1)
        fin_scr[...] = acc_scr[...]
        return mx_next

    acc_scr[...] = jnp.zeros(acc_scr.shape, F32)
    fin_scr[...] = jnp.ones(fin_scr.shape, F32)
    load_queries(0)
    lax.fori_loop(0, nq, tile, logits(0, 0))
    finalize(nq - 1)


def _hg_masks():
    c = HG_CHUNK
    t = np.arange(c)[:, None]
    s = np.arange(c)[None, :]
    full = []
    for lvl in range(HG_LEVELS):
        h = 1 << lvl
        same = (t // (2 * h)) == (s // (2 * h))
        full.append(same & ((t & h) != 0) & ((s & h) == 0))
    low = ([full[l] for l in range(HG_LOW_LEVELS)] + [full[l].T for l in range(HG_LOW_LEVELS)]
           + [t == s])
    high = []
    for rev in (False, True):
        for lvl in range(HG_LOW_LEVELS, HG_LEVELS):
            h = 1 << lvl
            rows = np.arange(c)
            qrows = rows[(rows & h) == 0] if rev else rows[(rows & h) != 0]
            mk = full[lvl].T if rev else full[lvl]
            high.append(mk[qrows])
    return np.stack(low).astype(np.float32), np.stack(high).astype(np.float32)


def _hg_chunk(q, z, v, st, lb, mlow_ref, mhigh_ref, rev):
    c = HG_CHUNK
    oml = 1.0 - lb
    ez = jnp.exp(-jnp.abs(z))
    r = 1.0 / (1.0 + ez)
    er = ez * r
    pos = z >= 0.0
    f = lb + oml * jnp.where(pos, r, er)
    k = oml * jnp.where(pos, er, r)
    row = lax.broadcasted_iota(jnp.int32, (c, HEAD_W), 0)
    a_in = f
    e_ex = jnp.ones_like(f)
    tot = f
    sc = jnp.sum(q * k, axis=1, keepdims=True) * mlow_ref[2 * HG_LOW_LEVELS]
    for lvl in range(HG_LOW_LEVELS):
        h = 1 << lvl
        bit = (row & h) != 0
        qside = jnp.logical_not(bit) if rev else bit
        x = jnp.where(qside, q * a_in, k * e_ex)
        lv = jnp.dot(x.astype(BF16), x.T.astype(BF16), preferred_element_type=F32)
        sc = sc + lv * mlow_ref[(HG_LOW_LEVELS if rev else 0) + lvl]
        tot3 = tot.reshape(HG_TILES, SUBLANES, HEAD_W)
        prev = pltpu.roll(tot3, h, 1).reshape(c, HEAD_W)
        nxt = pltpu.roll(tot3, SUBLANES - h, 1).reshape(c, HEAD_W)
        if rev:
            a_in = a_in * jnp.where(bit, 1.0, nxt)
            e_ex = e_ex * jnp.where(bit, prev, 1.0)
        else:
            a_in = a_in * jnp.where(bit, prev, 1.0)
            e_ex = e_ex * jnp.where(bit, 1.0, nxt)
        tot = tot * jnp.where(bit, prev, nxt)

    tiles = lambda arr: [arr[i * SUBLANES:(i + 1) * SUBLANES] for i in range(HG_TILES)]
    q_t, k_t, a_t, e_t, sc_t = tiles(q), tiles(k), tiles(a_in), tiles(e_ex), tiles(sc)
    tb = tiles(tot)
    for lvl in range(HG_LOW_LEVELS, HG_LEVELS):
        ht = (1 << lvl) // SUBLANES
        n_blocks = HG_TILES // (2 * ht)
        x_t = [None] * HG_TILES
        q_idx = []
        for b in range(n_blocks):
            lo = range(2 * b * ht, (2 * b + 1) * ht)
            hi = range((2 * b + 1) * ht, (2 * b + 2) * ht)
            q_half, k_half = (lo, hi) if rev else (hi, lo)
            for i in q_half:
                x_t[i] = q_t[i] * a_t[i]
            for i in k_half:
                x_t[i] = k_t[i] * e_t[i]
            q_idx.extend(q_half)
        x = jnp.concatenate(x_t, axis=0)
        xq = jnp.concatenate([x_t[i] for i in q_idx], axis=0).astype(BF16)
        lv = jnp.dot(xq, x.T.astype(BF16), preferred_element_type=F32)
        mk = mhigh_ref[(HG_LEVELS - HG_LOW_LEVELS if rev else 0) + lvl - HG_LOW_LEVELS]
        for n, i in enumerate(q_idx):
            rows = slice(n * SUBLANES, (n + 1) * SUBLANES)
            sc_t[i] = sc_t[i] + lv[rows] * mk[rows]
        new_tb = []
        for b in range(n_blocks):
            t_lo, t_hi = tb[2 * b], tb[2 * b + 1]
            lo = range(2 * b * ht, (2 * b + 1) * ht)
            hi = range((2 * b + 1) * ht, (2 * b + 2) * ht)
            if rev:
                for i in lo:
                    a_t[i] = a_t[i] * t_hi
                for i in hi:
                    e_t[i] = e_t[i] * t_lo
            else:
                for i in hi:
                    a_t[i] = a_t[i] * t_lo
                for i in lo:
                    e_t[i] = e_t[i] * t_hi
            new_tb.append(t_lo * t_hi)
        tb = new_tb

    sc = jnp.concatenate(sc_t, axis=0)
    qa = jnp.concatenate([q_t[i] * a_t[i] for i in range(HG_TILES)], axis=0)
    ke = jnp.concatenate([k_t[i] * e_t[i] for i in range(HG_TILES)], axis=0)
    vb = v.astype(BF16)
    intra = jnp.dot(sc.astype(BF16), vb, preferred_element_type=F32)
    inter = jnp.dot(qa.astype(BF16), st.astype(BF16), preferred_element_type=F32)
    upd = jnp.dot(ke.T.astype(BF16), vb, preferred_element_type=F32)
    st = st * tb[0].T[:, 0:1] + upd
    return inter + intra, st


def _mixer_kernel(lam_ref, q_ref, k_ref, vt_ref, ag_ref, hq_ref, zf_ref, zb_ref, hv_ref, hg_ref,
                  lb_ref, hgain_ref, mlow_ref, mhigh_ref, ao_ref, ho_ref,
                  s_scr, acc_scr, fin_scr, qt_scr, of_scr, ob_scr, st_scr,
                  *, tq, tk, tr, out_scale):
    seq = q_ref.shape[1]
    c = HG_CHUNK
    n = seq // c
    per_tile = n // (seq // tq)
    lb_f = lb_ref[0, 0:1, :]
    lb_b = lb_ref[0, 1:2, :]
    st_scr[...] = jnp.zeros(st_scr.shape, F32)

    def scans(u):
        st_f = st_scr[0]
        st_b = st_scr[1]
        for r in range(per_tile):
            i = u * per_tile + r
            cf = pl.multiple_of(i * c, c)
            cb = pl.multiple_of((n - 1 - i) * c, c)
            o_f, st_f = _hg_chunk(hq_ref[0, pl.ds(cf, c), :], zf_ref[0, pl.ds(cf, c), :],
                                  hv_ref[0, pl.ds(cf, c), :], st_f, lb_f, mlow_ref, mhigh_ref,
                                  False)
            of_scr[pl.ds(cf, c), :] = o_f
            o_b, st_b = _hg_chunk(hq_ref[0, pl.ds(cb, c), :], zb_ref[0, pl.ds(cb, c), :],
                                  hv_ref[0, pl.ds(cb, c), :], st_b, lb_b, mlow_ref, mhigh_ref,
                                  True)
            ob_scr[pl.ds(cb, c), :] = o_b
        st_scr[0] = st_f
        st_scr[1] = st_b

    _attn_body(lam_ref, q_ref, k_ref, vt_ref, ag_ref, ao_ref, s_scr, acc_scr, fin_scr, qt_scr,
               tq=tq, tk=tk, out_scale=out_scale, side_work=scans)

    gain = hgain_ref[0]

    def finish(i, carry):
        off = pl.multiple_of(i * tr, tr)
        o = of_scr[pl.ds(off, tr), :] + ob_scr[pl.ds(off, tr), :]
        y = _rms_scale(o, gain) * _sigmoid(hg_ref[0, pl.ds(off, tr), :])
        ho_ref[0, pl.ds(off, tr), :] = y.astype(ho_ref.dtype)
        return carry

    lax.fori_loop(0, seq // tr, finish, 0)


def _mixers(qk, vt, hg, lam, lbs, da_gains, hg_gains, layer, batch, seq, tq, tk, tr, out_scale):
    qk3 = qk.reshape(batch, seq, 2 * MIX_W)
    hg3 = hg.reshape(batch, seq, HG_W)
    nk = seq // tk
    assert nk % 2 == 0, "the attention pipeline walks key chunks in pairs"
    assert (seq // HG_CHUNK) % (seq // tq) == 0, "whole scan chunks per query tile"
    mlow, mhigh = (jnp.asarray(a) for a in _hg_masks())
    col = lambda g: (lambda b, h: (b, 0, g * N_HEADS + h))
    head_gain = pl.BlockSpec((None, 1, 1, HEAD_W), lambda b, h: (layer, h, 0, 0))
    seq_block = lambda g: pl.BlockSpec((1, seq, HEAD_W), col(g))
    out = jax.ShapeDtypeStruct((batch, seq, MIX_W), BF16)
    kern = functools.partial(_mixer_kernel, tq=tq, tk=tk, tr=tr, out_scale=out_scale)
    return pl.pallas_call(
        kern,
        out_shape=(out, out),
        grid=(batch, N_HEADS),
        in_specs=[pl.BlockSpec(memory_space=pltpu.SMEM),
                  seq_block(0),
                  seq_block(1),
                  pl.BlockSpec((nk, HEAD_W, tk), lambda b, h: (b, h, 0)),
                  head_gain,
                  seq_block(0), seq_block(1), seq_block(2), seq_block(3), seq_block(4),
                  pl.BlockSpec((1, 2, HEAD_W), lambda b, h: (h, 0, 0)),
                  head_gain,
                  pl.BlockSpec(mlow.shape, lambda b, h: (0, 0, 0)),
                  pl.BlockSpec(mhigh.shape, lambda b, h: (0, 0, 0))],
        out_specs=(pl.BlockSpec((1, seq, HEAD_W), lambda b, h: (b, 0, h)),
                   pl.BlockSpec((1, seq, HEAD_W), lambda b, h: (b, 0, h))),
        scratch_shapes=[pltpu.VMEM((2, tk, 2 * tq), F32),
                        pltpu.VMEM((HEAD_W + ONES_ROWS, 2 * tq), F32),
                        pltpu.VMEM((HEAD_W + ONES_ROWS, 2 * tq), F32),
                        pltpu.VMEM((HEAD_W, 2 * tq), BF16),
                        pltpu.VMEM((seq, HEAD_W), F32),
                        pltpu.VMEM((seq, HEAD_W), F32),
                        pltpu.VMEM((2, HEAD_W, HEAD_W), F32)],
        compiler_params=_cparams(("parallel", "parallel")),
        name="mixers",
    )(lam, qk3, qk3, vt, da_gains, hg3, hg3, hg3, hg3, hg3, lbs, hg_gains, mlow, mhigh)


def _post_kernel(a_ref, b_ref, gate_ref, x_ref, wa_ref, wb_ref, wo_ref, wg_ref, wu_ref, wd_ref,
                 g1_ref, g2_ref, *out_refs, emit_x):
    act_scr = out_refs[-1]
    ya = jnp.dot(a_ref[...], wa_ref[...].astype(BF16), preferred_element_type=F32)
    yb = jnp.dot(b_ref[...], wb_ref[...].astype(BF16), preferred_element_type=F32)
    merged = (gate_ref[:, :D_MODEL].astype(F32) * ya
              + gate_ref[:, D_MODEL:].astype(F32) * yb).astype(BF16)
    x_mid = x_ref[...] + jnp.dot(merged, wo_ref[...].astype(BF16), preferred_element_type=F32)
    h = _rms_scale(x_mid, g1_ref[...]).astype(BF16)
    for c in range(FFN_HIDDEN // MXU_N):
        sl = slice(c * MXU_N, (c + 1) * MXU_N)
        gate = jnp.dot(h, wg_ref[:, sl], preferred_element_type=F32)
        up = jnp.dot(h, wu_ref[:, sl], preferred_element_type=F32)
        act_scr[:, sl] = (gate * _sigmoid(gate) * up).astype(BF16)
    x_new = x_mid + jnp.dot(act_scr[...], wd_ref[...], preferred_element_type=F32)
    normed = _rms_scale(x_new, g2_ref[...])
    if emit_x:
        out_refs[0][...] = x_new
        out_refs[1][...] = normed.astype(out_refs[1].dtype)
    else:
        out_refs[0][...] = normed.astype(out_refs[0].dtype)


def _post(attn_o, hg_o, gates, x, wa, wb, wo, wg, wu, wd, layer, ffn_gains, next_gains,
          next_layer, tm, last):
    m = x.shape[0]
    row = lambda i: (i, 0)
    if last:
        out_shape = jax.ShapeDtypeStruct((m, D_MODEL), F32)
        out_specs = pl.BlockSpec((tm, D_MODEL), row)
    else:
        out_shape = (jax.ShapeDtypeStruct((m, D_MODEL), F32),
                     jax.ShapeDtypeStruct((m, D_MODEL), BF16))
        out_specs = (pl.BlockSpec((tm, D_MODEL), row), pl.BlockSpec((tm, D_MODEL), row))
    return pl.pallas_call(
        functools.partial(_post_kernel, emit_x=not last),
        out_shape=out_shape,
        grid=(m // tm,),
        in_specs=[pl.BlockSpec((tm, MIX_W), row),
                  pl.BlockSpec((tm, MIX_W), row),
                  pl.BlockSpec((tm, GATE_W), row),
                  pl.BlockSpec((tm, D_MODEL), row),
                  _resident_spec((MIX_W, D_MODEL), layer),
                  _resident_spec((MIX_W, D_MODEL), layer),
                  _resident_spec((D_MODEL, D_MODEL), layer),
                  _resident_spec((D_MODEL, FFN_HIDDEN), layer),
                  _resident_spec((D_MODEL, FFN_HIDDEN), layer),
                  _resident_spec((FFN_HIDDEN, D_MODEL), layer),
                  _layer_spec((1, D_MODEL), layer),
                  _layer_spec((1, D_MODEL), next_layer)],
        out_specs=out_specs,
        scratch_shapes=[pltpu.VMEM((tm, FFN_HIDDEN), BF16)],
        compiler_params=_cparams(("parallel",)),
        name="post",
    )(attn_o, hg_o, gates, x, wa, wb, wo, wg, wu, wd, ffn_gains, next_gains)


def _rope_expand():
    half = ROT_DIM // 2
    e = np.zeros((2 * half, 3 * HEAD_W), np.float32)
    base = np.zeros((1, 3 * HEAD_W), np.float32)
    for l in range(HEAD_W):
        d = l % QK_DIM
        if d < half:
            e[d, l] = 1.0
            e[half + d, HEAD_W + l] = -1.0
        elif d < ROT_DIM:
            e[d - half, l] = 1.0
            e[half + d - half, 2 * HEAD_W + l] = 1.0
        else:
            base[0, l] = 1.0
    return e, base


def _rope_tables(positions):
    inv_freq = ROPE_THETA ** (-(jnp.arange(0, ROT_DIM, 2, dtype=F32) / ROT_DIM))
    ang = positions.astype(F32).reshape(-1, 1) * inv_freq
    cs = jnp.concatenate([jnp.cos(ang), jnp.sin(ang)], axis=1)
    def head8(v):
        c = v * (2.0 ** 16 + 1.0)
        return c - (c - v)

    hi = head8(cs)
    mid = head8(cs - hi)
    lo = cs - hi - mid
    e, base = _rope_expand()
    e3 = jnp.asarray(np.concatenate([e, e, e], axis=0), BF16)
    pieces = jnp.concatenate([hi, mid, lo], axis=1).astype(BF16)
    return jnp.dot(pieces, e3, preferred_element_type=F32) + jnp.asarray(base)


def kernel(x, positions, w_in, da_lambda, da_norm, hg_lb_logits, hg_norm, w_a, w_b, w_o,
           attn_norm, ffn_norm, w_gate, w_up, w_down, final_norm):
    batch, seq, _ = x.shape
    depth = w_in.shape[0]
    m = batch * seq
    tm = min(512, seq)
    tq = tk = tm

    rope_t = _rope_tables(positions)
    p = jax.nn.softmax(hg_lb_logits.astype(F32), axis=1)
    csum = jnp.cumsum(p, axis=1)
    lbs = csum - csum[:, :1]

    w_gate, w_up, w_down = (w.astype(BF16) for w in (w_gate, w_up, w_down))
    w_in, w_a, w_b, w_o = (w.astype(F32) for w in (w_in, w_a, w_b, w_o))
    attn_gains = attn_norm.astype(F32).reshape(depth, 1, D_MODEL)
    ffn_gains = ffn_norm.astype(F32).reshape(depth, 1, D_MODEL)
    final_gain = final_norm.astype(F32).reshape(1, 1, D_MODEL)
    da_gains = da_norm.astype(F32).reshape(depth, N_HEADS, 1, HEAD_W)
    hg_gains = hg_norm.astype(F32).reshape(depth, N_HEADS, 1, HEAD_W)

    xf = x.reshape(m, D_MODEL).astype(F32)
    h = None
    out = None
    for layer in range(depth):
        lam_init = 0.8 - 0.6 * math.exp(-0.3 * layer)
        l32 = da_lambda[layer].astype(F32)
        lam_full = (jnp.exp(jnp.sum(l32[0] * l32[1])) - jnp.exp(jnp.sum(l32[2] * l32[3]))
                    + lam_init)
        lam = lam_full.reshape(1).astype(F32)
        lb_layer = lbs[:, layer].reshape(2, N_HEADS, HEAD_W).transpose(1, 0, 2)

        if layer == 0:
            qk, vt, hg, gates = _in_proj(xf, w_in, layer, rope_t, tm, gains=attn_gains)
        else:
            qk, vt, hg, gates = _in_proj(h, w_in, layer, rope_t, tm)
        attn_o, hg_o = _mixers(qk, vt, hg, lam, lb_layer, da_gains, hg_gains, layer, batch, seq,
                               tq, tk, min(512, seq), 1.0 - lam_init)
        last = layer == depth - 1
        args = (attn_o.reshape(m, MIX_W), hg_o.reshape(m, MIX_W), gates, xf,
                w_a, w_b, w_o, w_gate, w_up, w_down, layer, ffn_gains)
        if last:
            out = _post(*args, final_gain, 0, tm, True)
        else:
            xf, h = _post(*args, attn_gains, layer + 1, tm, False)
    return out.reshape(batch, seq, D_MODEL).astype(x.dtype)
```

```python
import functools
import math

import numpy as np
import jax
import jax.numpy as jnp
from jax import lax
from jax.experimental import pallas as pl
from jax.experimental.pallas import tpu as pltpu

F32 = jnp.float32
BF16 = jnp.bfloat16

D_MODEL = 1024
N_HEADS = 4
HEAD_W = 128
QK_DIM = 64
ROT_DIM = 16
ROPE_THETA = 500000.0
FFN_HIDDEN = 2816
NORM_EPS = 1e-6
MIX_W = N_HEADS * HEAD_W
HG_W = 5 * MIX_W
GATE_W = 2 * D_MODEL
IN_WIDTH = 3 * MIX_W + HG_W + GATE_W

LANES = 128
SUBLANES = 8
MXU_N = 256
VMEM_LIMIT = 56 * 1024 * 1024

HG_CHUNK = 128
HG_LEVELS = HG_CHUNK.bit_length() - 1
HG_LOW_LEVELS = SUBLANES.bit_length() - 1
HG_TILES = HG_CHUNK // SUBLANES

Q_SCALE = QK_DIM ** -0.5 * math.log2(math.e)
NT_DIMS = (((1,), (1,)), ((), ()))
ONES_ROWS = 16


def _cparams(sem):
    return pltpu.CompilerParams(dimension_semantics=sem, vmem_limit_bytes=VMEM_LIMIT)


def _sigmoid(x):
    return 1.0 / (1.0 + jnp.exp(-x))


def _rms_scale(x, gain):
    ms = jnp.mean(x * x, axis=-1, keepdims=True)
    return x * lax.rsqrt(ms + NORM_EPS) * gain


def _layer_spec(shape, layer):
    zeros = (0,) * len(shape)
    return pl.BlockSpec((None,) + tuple(shape), lambda *_: (layer,) + zeros)


def _resident_spec(shape, layer):
    zeros = (0,) * len(shape)
    return pl.BlockSpec((None,) + tuple(shape), lambda *_: (layer,) + zeros,
                        pipeline_mode=pl.Buffered(1))


def _proj_kernel(h_ref, w_ref, rope_ref, *rest, norm_input):
    if norm_input:
        g_ref, qk_ref, vt_ref, hg_ref, gate_ref = rest
        h = _rms_scale(h_ref[...], g_ref[...]).astype(BF16)
    else:
        qk_ref, vt_ref, hg_ref, gate_ref = rest
        h = h_ref[...]
    cos_t = rope_ref[:, 0:HEAD_W]
    sin_a = rope_ref[:, HEAD_W:2 * HEAD_W]
    sin_b = rope_ref[:, 2 * HEAD_W:3 * HEAD_W]

    def rope(a, scale):
        outs = []
        for hh in range(N_HEADS):
            t = a[:, hh * HEAD_W:(hh + 1) * HEAD_W]
            r = (t * cos_t + pltpu.roll(t, HEAD_W - ROT_DIM // 2, 1) * sin_a
                 + pltpu.roll(t, ROT_DIM // 2, 1) * sin_b)
            outs.append(r * scale if scale != 1.0 else r)
        return jnp.concatenate(outs, axis=1)

    n_groups = IN_WIDTH // MIX_W
    for g in range(n_groups):
        acc = jnp.dot(h, w_ref[:, g * MIX_W:(g + 1) * MIX_W].astype(BF16),
                      preferred_element_type=F32)
        if g == 0:
            qk_ref[:, 0:MIX_W] = rope(acc, Q_SCALE).astype(BF16)
        elif g == 1:
            qk_ref[:, MIX_W:2 * MIX_W] = rope(acc, 1.0).astype(BF16)
        elif g == 2:
            vt_ref[0] = acc.T.astype(BF16)
        elif g < 8:
            hg_ref[:, (g - 3) * MIX_W:(g - 2) * MIX_W] = acc
        else:
            gate_ref[:, (g - 8) * MIX_W:(g - 7) * MIX_W] = _sigmoid(acc).astype(BF16)


def _in_proj(h, w_all, layer, rope_t, tm, gains=None):
    m = h.shape[0]
    row = lambda i: (i, 0)
    extra_specs = [] if gains is None else [_layer_spec((1, D_MODEL), layer)]
    extra_args = [] if gains is None else [gains]
    return pl.pallas_call(
        functools.partial(_proj_kernel, norm_input=gains is not None),
        out_shape=(jax.ShapeDtypeStruct((m, 2 * MIX_W), BF16),
                   jax.ShapeDtypeStruct((m // tm, MIX_W, tm), BF16),
                   jax.ShapeDtypeStruct((m, HG_W), F32),
                   jax.ShapeDtypeStruct((m, GATE_W), BF16)),
        grid=(m // tm,),
        in_specs=[pl.BlockSpec((tm, D_MODEL), row),
                  _resident_spec((D_MODEL, IN_WIDTH), layer),
                  pl.BlockSpec((tm, 3 * HEAD_W), row)] + extra_specs,
        out_specs=(pl.BlockSpec((tm, 2 * MIX_W), row),
                   pl.BlockSpec((1, MIX_W, tm), lambda i: (i, 0, 0)),
                   pl.BlockSpec((tm, HG_W), row),
                   pl.BlockSpec((tm, GATE_W), row)),
        compiler_params=_cparams(("parallel",)),
        name="in_proj",
    )(h, w_all, rope_t, *extra_args)


def _attn_body(lam_ref, q_ref, k_ref, vt_ref, g_ref, o_ref, s_scr, acc_scr, fin_scr,
               *, tq, tk, out_scale, side_work):
    seq = k_ref.shape[1]
    nk = seq // tk
    nq = seq // tq
    lane = lax.broadcasted_iota(jnp.int32, (tq, HEAD_W), 1)
    ones = jnp.ones((ONES_ROWS, tk), BF16)
    m0 = jnp.full((1, 2 * tq), -jnp.inf, F32)
    lam = lam_ref[0]
    gain = g_ref[0]

    def logits(u, j, slot):
        q = q_ref[0, pl.ds(pl.multiple_of(u * tq, tq), tq), :]
        zero = jnp.zeros_like(q)
        qs = jnp.concatenate([jnp.where(lane < QK_DIM, q, zero),
                              jnp.where(lane >= QK_DIM, q, zero)], axis=0)
        st = lax.dot_general(k_ref[0, pl.ds(pl.multiple_of(j * tk, tk), tk), :], qs, NT_DIMS,
                             preferred_element_type=F32)
        s_scr[slot] = st
        return jnp.max(st, axis=0, keepdims=True)

    def accumulate(j, slot, m_old, mx):
        m_new = jnp.maximum(m_old, mx)
        alpha = jnp.exp2(m_old - m_new)
        p = jnp.exp2(s_scr[slot] - m_new).astype(BF16)
        vt_ext = jnp.concatenate([vt_ref[j], ones], axis=0)
        acc_scr[...] = acc_scr[...] * alpha + jnp.dot(vt_ext, p, preferred_element_type=F32)
        return m_new

    def pair(jj, carry, u):
        m, mx = carry
        j = 2 * jj
        mx1 = logits(u, j + 1, 1)
        m = accumulate(j, 0, m, mx)
        mx2 = logits(u, j + 2, 0)
        m = accumulate(j + 1, 1, m, mx1)
        return m, mx2

    def finalize(u):
        acc = fin_scr[...]
        o_t = acc[:HEAD_W, :] / acc[HEAD_W:HEAD_W + 1, :]
        d_t = o_t[:, :tq] - lam * o_t[:, tq:]
        o_ref[0, pl.ds(pl.multiple_of(u * tq, tq), tq), :] = (
            _rms_scale(d_t.T, gain) * out_scale).astype(o_ref.dtype)

    def tile(u, mx):
        finalize(jnp.maximum(u - 1, 0))
        side_work(u)
        m, mx = lax.fori_loop(0, nk // 2 - 1, functools.partial(pair, u=u), (m0, mx), unroll=True)
        mx1 = logits(u, nk - 1, 1)
        m = accumulate(nk - 2, 0, m, mx)
        mx_next = logits(jnp.minimum(u + 1, nq - 1), 0, 0)
        accumulate(nk - 1, 1, m, mx1)
        fin_scr[...] = acc_scr[...]
        return mx_next

    acc_scr[...] = jnp.zeros(acc_scr.shape, F32)
    fin_scr[...] = jnp.ones(fin_scr.shape, F32)
    lax.fori_loop(0, nq, tile, logits(0, 0, 0))
    finalize(nq - 1)


def _hg_masks():
    c = HG_CHUNK
    t = np.arange(c)[:, None]
    s = np.arange(c)[None, :]
    full = []
    for lvl in range(HG_LEVELS):
        h = 1 << lvl
        same = (t // (2 * h)) == (s // (2 * h))
        full.append(same & ((t & h) != 0) & ((s & h) == 0))
    low = ([full[l] for l in range(HG_LOW_LEVELS)] + [full[l].T for l in range(HG_LOW_LEVELS)]
           + [t == s])
    high = []
    for rev in (False, True):
        for lvl in range(HG_LOW_LEVELS, HG_LEVELS):
            h = 1 << lvl
            rows = np.arange(c)
            qrows = rows[(rows & h) == 0] if rev else rows[(rows & h) != 0]
            mk = full[lvl].T if rev else full[lvl]
            high.append(mk[qrows])
    return np.stack(low).astype(np.float32), np.stack(high).astype(np.float32)


def _hg_chunk(q, z, v, st, lb, mlow_ref, mhigh_ref, rev):
    c = HG_CHUNK
    oml = 1.0 - lb
    ez = jnp.exp(-jnp.abs(z))
    r = 1.0 / (1.0 + ez)
    er = ez * r
    pos = z >= 0.0
    f = lb + oml * jnp.where(pos, r, er)
    k = oml * jnp.where(pos, er, r)
    row = lax.broadcasted_iota(jnp.int32, (c, HEAD_W), 0)
    a_in = f
    e_ex = jnp.ones_like(f)
    tot = f
    sc = jnp.sum(q * k, axis=1, keepdims=True) * mlow_ref[2 * HG_LOW_LEVELS]
    for lvl in range(HG_LOW_LEVELS):
        h = 1 << lvl
        bit = (row & h) != 0
        qside = jnp.logical_not(bit) if rev else bit
        x = jnp.where(qside, q * a_in, k * e_ex)
        lv = jnp.dot(x.astype(BF16), x.T.astype(BF16), preferred_element_type=F32)
        sc = sc + lv * mlow_ref[(HG_LOW_LEVELS if rev else 0) + lvl]
        tot3 = tot.reshape(HG_TILES, SUBLANES, HEAD_W)
        prev = pltpu.roll(tot3, h, 1).reshape(c, HEAD_W)
        nxt = pltpu.roll(tot3, SUBLANES - h, 1).reshape(c, HEAD_W)
        if rev:
            a_in = a_in * jnp.where(bit, 1.0, nxt)
            e_ex = e_ex * jnp.where(bit, prev, 1.0)
        else:
            a_in = a_in * jnp.where(bit, prev, 1.0)
            e_ex = e_ex * jnp.where(bit, 1.0, nxt)
        tot = tot * jnp.where(bit, prev, nxt)

    tiles = lambda arr: [arr[i * SUBLANES:(i + 1) * SUBLANES] for i in range(HG_TILES)]
    q_t, k_t, a_t, e_t, sc_t = tiles(q), tiles(k), tiles(a_in), tiles(e_ex), tiles(sc)
    tb = tiles(tot)
    for lvl in range(HG_LOW_LEVELS, HG_LEVELS):
        ht = (1 << lvl) // SUBLANES
        n_blocks = HG_TILES // (2 * ht)
        x_t = [None] * HG_TILES
        q_idx = []
        for b in range(n_blocks):
            lo = range(2 * b * ht, (2 * b + 1) * ht)
            hi = range((2 * b + 1) * ht, (2 * b + 2) * ht)
            q_half, k_half = (lo, hi) if rev else (hi, lo)
            for i in q_half:
                x_t[i] = q_t[i] * a_t[i]
            for i in k_half:
                x_t[i] = k_t[i] * e_t[i]
            q_idx.extend(q_half)
        x = jnp.concatenate(x_t, axis=0)
        xq = jnp.concatenate([x_t[i] for i in q_idx], axis=0).astype(BF16)
        lv = jnp.dot(xq, x.T.astype(BF16), preferred_element_type=F32)
        mk = mhigh_ref[(HG_LEVELS - HG_LOW_LEVELS if rev else 0) + lvl - HG_LOW_LEVELS]
        for n, i in enumerate(q_idx):
            rows = slice(n * SUBLANES, (n + 1) * SUBLANES)
            sc_t[i] = sc_t[i] + lv[rows] * mk[rows]
        new_tb = []
        for b in range(n_blocks):
            t_lo, t_hi = tb[2 * b], tb[2 * b + 1]
            lo = range(2 * b * ht, (2 * b + 1) * ht)
            hi = range((2 * b + 1) * ht, (2 * b + 2) * ht)
            if rev:
                for i in lo:
                    a_t[i] = a_t[i] * t_hi
                for i in hi:
                    e_t[i] = e_t[i] * t_lo
            else:
                for i in hi:
                    a_t[i] = a_t[i] * t_lo
                for i in lo:
                    e_t[i] = e_t[i] * t_hi
            new_tb.append(t_lo * t_hi)
        tb = new_tb

    sc = jnp.concatenate(sc_t, axis=0)
    qa = jnp.concatenate([q_t[i] * a_t[i] for i in range(HG_TILES)], axis=0)
    ke = jnp.concatenate([k_t[i] * e_t[i] for i in range(HG_TILES)], axis=0)
    vb = v.astype(BF16)
    both = jnp.dot(jnp.concatenate([sc.astype(BF16), ke.T.astype(BF16)], axis=0), vb,
                   preferred_element_type=F32)
    inter = jnp.dot(qa.astype(BF16), st.astype(BF16), preferred_element_type=F32)
    st = st * tb[0].T[:, 0:1] + both[c:]
    return inter + both[:c], st


def _mixer_kernel(lam_ref, q_ref, k_ref, vt_ref, ag_ref, hq_ref, zf_ref, zb_ref, hv_ref, hg_ref,
                  lb_ref, hgain_ref, mlow_ref, mhigh_ref, ao_ref, ho_ref,
                  s_scr, acc_scr, fin_scr, of_scr, ob_scr, st_scr, *, tq, tk, tr, out_scale):
    seq = q_ref.shape[1]
    c = HG_CHUNK
    n = seq // c
    per_tile = n // (seq // tq)
    lb_f = lb_ref[0, 0:1, :]
    lb_b = lb_ref[0, 1:2, :]
    st_scr[...] = jnp.zeros(st_scr.shape, F32)

    def scans(u):
        st_f = st_scr[0]
        st_b = st_scr[1]
        for r in range(per_tile):
            i = u * per_tile + r
            cf = pl.multiple_of(i * c, c)
            cb = pl.multiple_of((n - 1 - i) * c, c)
            o_f, st_f = _hg_chunk(hq_ref[0, pl.ds(cf, c), :], zf_ref[0, pl.ds(cf, c), :],
                                  hv_ref[0, pl.ds(cf, c), :], st_f, lb_f, mlow_ref, mhigh_ref,
                                  False)
            of_scr[pl.ds(cf, c), :] = o_f
            o_b, st_b = _hg_chunk(hq_ref[0, pl.ds(cb, c), :], zb_ref[0, pl.ds(cb, c), :],
                                  hv_ref[0, pl.ds(cb, c), :], st_b, lb_b, mlow_ref, mhigh_ref,
                                  True)
            ob_scr[pl.ds(cb, c), :] = o_b
        st_scr[0] = st_f
        st_scr[1] = st_b

    _attn_body(lam_ref, q_ref, k_ref, vt_ref, ag_ref, ao_ref, s_scr, acc_scr, fin_scr,
               tq=tq, tk=tk, out_scale=out_scale, side_work=scans)

    gain = hgain_ref[0]

    def finish(i, carry):
        off = pl.multiple_of(i * tr, tr)
        o = of_scr[pl.ds(off, tr), :] + ob_scr[pl.ds(off, tr), :]
        y = _rms_scale(o, gain) * _sigmoid(hg_ref[0, pl.ds(off, tr), :])
        ho_ref[0, pl.ds(off, tr), :] = y.astype(ho_ref.dtype)
        return carry

    lax.fori_loop(0, seq // tr, finish, 0)


def _mixers(qk, vt, hg, lam, lbs, da_gains, hg_gains, layer, batch, seq, tq, tk, tr, out_scale):
    qk3 = qk.reshape(batch, seq, 2 * MIX_W)
    hg3 = hg.reshape(batch, seq, HG_W)
    nk = seq // tk
    assert nk % 2 == 0, "the attention pipeline walks key chunks in pairs"
    assert (seq // HG_CHUNK) % (seq // tq) == 0, "whole scan chunks per query tile"
    mlow, mhigh = (jnp.asarray(a) for a in _hg_masks())
    col = lambda g: (lambda b, h: (b, 0, g * N_HEADS + h))
    head_gain = pl.BlockSpec((None, 1, 1, HEAD_W), lambda b, h: (layer, h, 0, 0))
    seq_block = lambda g: pl.BlockSpec((1, seq, HEAD_W), col(g))
    out = jax.ShapeDtypeStruct((batch, seq, MIX_W), BF16)
    kern = functools.partial(_mixer_kernel, tq=tq, tk=tk, tr=tr, out_scale=out_scale)
    return pl.pallas_call(
        kern,
        out_shape=(out, out),
        grid=(batch, N_HEADS),
        in_specs=[pl.BlockSpec(memory_space=pltpu.SMEM),
                  seq_block(0),
                  seq_block(1),
                  pl.BlockSpec((nk, HEAD_W, tk), lambda b, h: (b, h, 0)),
                  head_gain,
                  seq_block(0), seq_block(1), seq_block(2), seq_block(3), seq_block(4),
                  pl.BlockSpec((1, 2, HEAD_W), lambda b, h: (h, 0, 0)),
                  head_gain,
                  pl.BlockSpec(mlow.shape, lambda b, h: (0, 0, 0)),
                  pl.BlockSpec(mhigh.shape, lambda b, h: (0, 0, 0))],
        out_specs=(pl.BlockSpec((1, seq, HEAD_W), lambda b, h: (b, 0, h)),
                   pl.BlockSpec((1, seq, HEAD_W), lambda b, h: (b, 0, h))),
        scratch_shapes=[pltpu.VMEM((2, tk, 2 * tq), F32),
                        pltpu.VMEM((HEAD_W + ONES_ROWS, 2 * tq), F32),
                        pltpu.VMEM((HEAD_W + ONES_ROWS, 2 * tq), F32),
                        pltpu.VMEM((seq, HEAD_W), F32),
                        pltpu.VMEM((seq, HEAD_W), F32),
                        pltpu.VMEM((2, HEAD_W, HEAD_W), F32)],
        compiler_params=_cparams(("parallel", "parallel")),
        name="mixers",
    )(lam, qk3, qk3, vt, da_gains, hg3, hg3, hg3, hg3, hg3, lbs, hg_gains, mlow, mhigh)


def _post_kernel(a_ref, b_ref, gate_ref, x_ref, wa_ref, wb_ref, wo_ref, wg_ref, wu_ref, wd_ref,
                 g1_ref, g2_ref, *out_refs, emit_x):
    act_scr = out_refs[-1]
    ya = jnp.dot(a_ref[...], wa_ref[...].astype(BF16), preferred_element_type=F32)
    yb = jnp.dot(b_ref[...], wb_ref[...].astype(BF16), preferred_element_type=F32)
    merged = (gate_ref[:, :D_MODEL].astype(F32) * ya
              + gate_ref[:, D_MODEL:].astype(F32) * yb).astype(BF16)
    x_mid = x_ref[...] + jnp.dot(merged, wo_ref[...].astype(BF16), preferred_element_type=F32)
    h = _rms_scale(x_mid, g1_ref[...]).astype(BF16)
    for c in range(FFN_HIDDEN // MXU_N):
        sl = slice(c * MXU_N, (c + 1) * MXU_N)
        gate = jnp.dot(h, wg_ref[:, sl], preferred_element_type=F32)
        up = jnp.dot(h, wu_ref[:, sl], preferred_element_type=F32)
        act_scr[:, sl] = (gate * _sigmoid(gate) * up).astype(BF16)
    x_new = x_mid + jnp.dot(act_scr[...], wd_ref[...], preferred_element_type=F32)
    normed = _rms_scale(x_new, g2_ref[...])
    if emit_x:
        out_refs[0][...] = x_new
        out_refs[1][...] = normed.astype(out_refs[1].dtype)
    else:
        out_refs[0][...] = normed.astype(out_refs[0].dtype)


def _post(attn_o, hg_o, gates, x, wa, wb, wo, wg, wu, wd, layer, ffn_gains, next_gains,
          next_layer, tm, last):
    m = x.shape[0]
    row = lambda i: (i, 0)
    if last:
        out_shape = jax.ShapeDtypeStruct((m, D_MODEL), F32)
        out_specs = pl.BlockSpec((tm, D_MODEL), row)
    else:
        out_shape = (jax.ShapeDtypeStruct((m, D_MODEL), F32),
                     jax.ShapeDtypeStruct((m, D_MODEL), BF16))
        out_specs = (pl.BlockSpec((tm, D_MODEL), row), pl.BlockSpec((tm, D_MODEL), row))
    return pl.pallas_call(
        functools.partial(_post_kernel, emit_x=not last),
        out_shape=out_shape,
        grid=(m // tm,),
        in_specs=[pl.BlockSpec((tm, MIX_W), row),
                  pl.BlockSpec((tm, MIX_W), row),
                  pl.BlockSpec((tm, GATE_W), row),
                  pl.BlockSpec((tm, D_MODEL), row),
                  _resident_spec((MIX_W, D_MODEL), layer),
                  _resident_spec((MIX_W, D_MODEL), layer),
                  _resident_spec((D_MODEL, D_MODEL), layer),
                  _resident_spec((D_MODEL, FFN_HIDDEN), layer),
                  _resident_spec((D_MODEL, FFN_HIDDEN), layer),
                  _resident_spec((FFN_HIDDEN, D_MODEL), layer),
                  _layer_spec((1, D_MODEL), layer),
                  _layer_spec((1, D_MODEL), next_layer)],
        out_specs=out_specs,
        scratch_shapes=[pltpu.VMEM((tm, FFN_HIDDEN), BF16)],
        compiler_params=_cparams(("parallel",)),
        name="post",
    )(attn_o, hg_o, gates, x, wa, wb, wo, wg, wu, wd, ffn_gains, next_gains)


def _rope_expand():
    half = ROT_DIM // 2
    e = np.zeros((2 * half, 3 * HEAD_W), np.float32)
    base = np.zeros((1, 3 * HEAD_W), np.float32)
    for l in range(HEAD_W):
        d = l % QK_DIM
        if d < half:
            e[d, l] = 1.0
            e[half + d, HEAD_W + l] = -1.0
        elif d < ROT_DIM:
            e[d - half, l] = 1.0
            e[half + d - half, 2 * HEAD_W + l] = 1.0
        else:
            base[0, l] = 1.0
    return e, base


def _rope_tables(positions):
    inv_freq = ROPE_THETA ** (-(jnp.arange(0, ROT_DIM, 2, dtype=F32) / ROT_DIM))
    ang = positions.astype(F32).reshape(-1, 1) * inv_freq
    cs = jnp.concatenate([jnp.cos(ang), jnp.sin(ang)], axis=1)
    def head8(v):
        c = v * (2.0 ** 16 + 1.0)
        return c - (c - v)

    hi = head8(cs)
    mid = head8(cs - hi)
    lo = cs - hi - mid
    e, base = _rope_expand()
    e3 = jnp.asarray(np.concatenate([e, e, e], axis=0), BF16)
    pieces = jnp.concatenate([hi, mid, lo], axis=1).astype(BF16)
    return jnp.dot(pieces, e3, preferred_element_type=F32) + jnp.asarray(base)


def kernel(x, positions, w_in, da_lambda, da_norm, hg_lb_logits, hg_norm, w_a, w_b, w_o,
           attn_norm, ffn_norm, w_gate, w_up, w_down, final_norm):
    batch, seq, _ = x.shape
    depth = w_in.shape[0]
    m = batch * seq
    tm = min(512, seq)
    tq = tk = tm

    rope_t = _rope_tables(positions)
    p = jax.nn.softmax(hg_lb_logits.astype(F32), axis=1)
    csum = jnp.cumsum(p, axis=1)
    lbs = csum - csum[:, :1]

    w_gate, w_up, w_down = (w.astype(BF16) for w in (w_gate, w_up, w_down))
    w_in, w_a, w_b, w_o = (w.astype(F32) for w in (w_in, w_a, w_b, w_o))
    attn_gains = attn_norm.astype(F32).reshape(depth, 1, D_MODEL)
    ffn_gains = ffn_norm.astype(F32).reshape(depth, 1, D_MODEL)
    final_gain = final_norm.astype(F32).reshape(1, 1, D_MODEL)
    da_gains = da_norm.astype(F32).reshape(depth, N_HEADS, 1, HEAD_W)
    hg_gains = hg_norm.astype(F32).reshape(depth, N_HEADS, 1, HEAD_W)

    xf = x.reshape(m, D_MODEL).astype(F32)
    h = None
    out = None
    for layer in range(depth):
        lam_init = 0.8 - 0.6 * math.exp(-0.3 * layer)
        l32 = da_lambda[layer].astype(F32)
        lam_full = (jnp.exp(jnp.sum(l32[0] * l32[1])) - jnp.exp(jnp.sum(l32[2] * l32[3]))
                    + lam_init)
        lam = lam_full.reshape(1).astype(F32)
        lb_layer = lbs[:, layer].reshape(2, N_HEADS, HEAD_W).transpose(1, 0, 2)

        if layer == 0:
            qk, vt, hg, gates = _in_proj(xf, w_in, layer, rope_t, tm, gains=attn_gains)
        else:
            qk, vt, hg, gates = _in_proj(h, w_in, layer, rope_t, tm)
        attn_o, hg_o = _mixers(qk, vt, hg, lam, lb_layer, da_gains, hg_gains, layer, batch, seq,
                               tq, tk, min(512, seq), 1.0 - lam_init)
        last = layer == depth - 1
        args = (attn_o.reshape(m, MIX_W), hg_o.reshape(m, MIX_W), gates, xf,
                w_a, w_b, w_o, w_gate, w_up, w_down, layer, ffn_gains)
        if last:
            out = _post(*args, final_gain, 0, tm, True)
        else:
            xf, h = _post(*args, attn_gains, layer + 1, tm, False)
    return out.reshape(batch, seq, D_MODEL).astype(x.dtype)
```

```python
import functools
import math

import numpy as np
import jax
import jax.numpy as jnp
from jax import lax
from jax.experimental import pallas as pl
from jax.experimental.pallas import tpu as pltpu

F32 = jnp.float32
BF16 = jnp.bfloat16

D_MODEL = 1024
N_HEADS = 4
HEAD_W = 128
QK_DIM = 64
ROT_DIM = 16
ROPE_THETA = 500000.0
FFN_HIDDEN = 2816
NORM_EPS = 1e-6
MIX_W = N_HEADS * HEAD_W
HG_W = 5 * MIX_W
GATE_W = 2 * D_MODEL
IN_WIDTH = 3 * MIX_W + HG_W + GATE_W

LANES = 128
SUBLANES = 8
MXU_N = 256
VMEM_LIMIT = 56 * 1024 * 1024

HG_CHUNK = 128
HG_LEVELS = HG_CHUNK.bit_length() - 1
HG_LOW_LEVELS = SUBLANES.bit_length() - 1
HG_TILES = HG_CHUNK // SUBLANES

NT_DIMS = (((1,), (1,)), ((), ()))
Q_SCALE = QK_DIM ** -0.5 * math.log2(math.e)
ONES_ROWS = 16


def _cparams(sem):
    return pltpu.CompilerParams(dimension_semantics=sem, vmem_limit_bytes=VMEM_LIMIT)


def _sigmoid(x):
    return 1.0 / (1.0 + jnp.exp(-x))


def _rms_scale(x, gain):
    ms = jnp.mean(x * x, axis=-1, keepdims=True)
    return x * lax.rsqrt(ms + NORM_EPS) * gain


def _layer_spec(shape, layer):
    zeros = (0,) * len(shape)
    return pl.BlockSpec((None,) + tuple(shape), lambda *_: (layer,) + zeros)


def _resident_spec(shape, layer):
    zeros = (0,) * len(shape)
    return pl.BlockSpec((None,) + tuple(shape), lambda *_: (layer,) + zeros,
                        pipeline_mode=pl.Buffered(1))


def _proj_kernel(h_ref, w_ref, rope_ref, *rest, norm_input):
    if norm_input:
        g_ref, qk_ref, vt_ref, hg_ref, gate_ref = rest
        h = _rms_scale(h_ref[...], g_ref[...]).astype(BF16)
    else:
        qk_ref, vt_ref, hg_ref, gate_ref = rest
        h = h_ref[...]
    cos_t = rope_ref[:, 0:HEAD_W]
    sin_a = rope_ref[:, HEAD_W:2 * HEAD_W]
    sin_b = rope_ref[:, 2 * HEAD_W:3 * HEAD_W]

    def rope(a, scale):
        outs = []
        for hh in range(N_HEADS):
            t = a[:, hh * HEAD_W:(hh + 1) * HEAD_W]
            r = (t * cos_t + pltpu.roll(t, HEAD_W - ROT_DIM // 2, 1) * sin_a
                 + pltpu.roll(t, ROT_DIM // 2, 1) * sin_b)
            outs.append(r * scale if scale != 1.0 else r)
        return jnp.concatenate(outs, axis=1)

    n_groups = IN_WIDTH // MIX_W
    for g in range(n_groups):
        acc = jnp.dot(h, w_ref[:, g * MIX_W:(g + 1) * MIX_W].astype(BF16),
                      preferred_element_type=F32)
        if g == 0:
            qk_ref[:, 0:MIX_W] = rope(acc, Q_SCALE).astype(BF16)
        elif g == 1:
            qk_ref[:, MIX_W:2 * MIX_W] = rope(acc, 1.0).astype(BF16)
        elif g == 2:
            vt_ref[0] = acc.T.astype(BF16)
        elif g < 8:
            hg_ref[:, (g - 3) * MIX_W:(g - 2) * MIX_W] = acc
        else:
            gate_ref[:, (g - 8) * MIX_W:(g - 7) * MIX_W] = _sigmoid(acc).astype(BF16)


def _in_proj(h, w_all, layer, rope_t, tm, gains=None):
    m = h.shape[0]
    row = lambda i: (i, 0)
    extra_specs = [] if gains is None else [_layer_spec((1, D_MODEL), layer)]
    extra_args = [] if gains is None else [gains]
    return pl.pallas_call(
        functools.partial(_proj_kernel, norm_input=gains is not None),
        out_shape=(jax.ShapeDtypeStruct((m, 2 * MIX_W), BF16),
                   jax.ShapeDtypeStruct((m // tm, MIX_W, tm), BF16),
                   jax.ShapeDtypeStruct((m, HG_W), F32),
                   jax.ShapeDtypeStruct((m, GATE_W), BF16)),
        grid=(m // tm,),
        in_specs=[pl.BlockSpec((tm, D_MODEL), row),
                  _resident_spec((D_MODEL, IN_WIDTH), layer),
                  pl.BlockSpec((tm, 3 * HEAD_W), row)] + extra_specs,
        out_specs=(pl.BlockSpec((tm, 2 * MIX_W), row),
                   pl.BlockSpec((1, MIX_W, tm), lambda i: (i, 0, 0)),
                   pl.BlockSpec((tm, HG_W), row),
                   pl.BlockSpec((tm, GATE_W), row)),
        compiler_params=_cparams(("parallel",)),
        name="in_proj",
    )(h, w_all, rope_t, *extra_args)


def _attn_body(lam_ref, q_ref, k_ref, vt_ref, g_ref, o_ref, s_scr, acc_scr, fin_scr,
               *, tq, tk, out_scale, side_work):
    seq = k_ref.shape[1]
    nk = seq // tk
    nq = seq // tq
    lane = lax.broadcasted_iota(jnp.int32, (tq, HEAD_W), 1)
    ones = jnp.ones((ONES_ROWS, tk), BF16)
    m0 = jnp.full((1, 2 * tq), -jnp.inf, F32)
    lam = lam_ref[0]
    gain = g_ref[0]

    def logits(u, j, slot):
        q = q_ref[0, pl.ds(pl.multiple_of(u * tq, tq), tq), :]
        zero = jnp.zeros_like(q)
        qs = jnp.concatenate([jnp.where(lane < QK_DIM, q, zero),
                              jnp.where(lane >= QK_DIM, q, zero)], axis=0)
        st = lax.dot_general(k_ref[0, pl.ds(pl.multiple_of(j * tk, tk), tk), :], qs, NT_DIMS,
                             preferred_element_type=F32)
        s_scr[slot] = st
        return jnp.max(st, axis=0, keepdims=True)

    def accumulate(j, slot, m_old, mx):
        m_new = jnp.maximum(m_old, mx)
        alpha = jnp.exp2(m_old - m_new)
        p = jnp.exp2(s_scr[slot] - m_new).astype(BF16)
        vt_ext = jnp.concatenate([vt_ref[j], ones], axis=0)
        acc_scr[...] = acc_scr[...] * alpha + jnp.dot(vt_ext, p, preferred_element_type=F32)
        return m_new

    def pair(jj, carry, u):
        m, mx = carry
        j = 2 * jj
        mx1 = logits(u, j + 1, 1)
        m = accumulate(j, 0, m, mx)
        mx2 = logits(u, j + 2, 0)
        m = accumulate(j + 1, 1, m, mx1)
        return m, mx2

    def finalize(u):
        acc = fin_scr[...]
        o_t = acc[:HEAD_W, :] / acc[HEAD_W:HEAD_W + 1, :]
        d_t = o_t[:, :tq] - lam * o_t[:, tq:]
        o_ref[0, pl.ds(pl.multiple_of(u * tq, tq), tq), :] = (
            _rms_scale(d_t.T, gain) * out_scale).astype(o_ref.dtype)

    def tile(u, mx):
        finalize(jnp.maximum(u - 1, 0))
        side_work(u)
        m, mx = lax.fori_loop(0, nk // 2 - 1, functools.partial(pair, u=u), (m0, mx), unroll=True)
        mx1 = logits(u, nk - 1, 1)
        m = accumulate(nk - 2, 0, m, mx)
        mx_next = logits(jnp.minimum(u + 1, nq - 1), 0, 0)
        accumulate(nk - 1, 1, m, mx1)
        fin_scr[...] = acc_scr[...]
        return mx_next

    acc_scr[...] = jnp.zeros(acc_scr.shape, F32)
    fin_scr[...] = jnp.ones(fin_scr.shape, F32)
    lax.fori_loop(0, nq, tile, logits(0, 0, 0))
    finalize(nq - 1)


def _hg_masks():
    c = HG_CHUNK
    t = np.arange(c)[:, None]
    s = np.arange(c)[None, :]
    full = []
    for lvl in range(HG_LEVELS):
        h = 1 << lvl
        same = (t // (2 * h)) == (s // (2 * h))
        full.append(same & ((t & h) != 0) & ((s & h) == 0))
    low = ([full[l] for l in range(HG_LOW_LEVELS)] + [full[l].T for l in range(HG_LOW_LEVELS)]
           + [t == s])
    high = []
    for rev in (False, True):
        for lvl in range(HG_LOW_LEVELS, HG_LEVELS):
            h = 1 << lvl
            rows = np.arange(c)
            qrows = rows[(rows & h) == 0] if rev else rows[(rows & h) != 0]
            mk = full[lvl].T if rev else full[lvl]
            high.append(mk[qrows])
    return np.stack(low).astype(np.float32), np.stack(high).astype(np.float32)


def _hg_chunk(q, z, v, st_t, lb, mlow_ref, mhigh_ref, rev):
    c = HG_CHUNK
    oml = 1.0 - lb
    ez = jnp.exp(-jnp.abs(z))
    r = 1.0 / (1.0 + ez)
    er = ez * r
    pos = z >= 0.0
    f = lb + oml * jnp.where(pos, r, er)
    k = oml * jnp.where(pos, er, r)
    row = lax.broadcasted_iota(jnp.int32, (c, HEAD_W), 0)
    a_in = f
    e_ex = jnp.ones_like(f)
    tot = f
    sc = jnp.sum(q * k, axis=1, keepdims=True) * mlow_ref[2 * HG_LOW_LEVELS]
    for lvl in range(HG_LOW_LEVELS):
        h = 1 << lvl
        bit = (row & h) != 0
        qside = jnp.logical_not(bit) if rev else bit
        x = jnp.where(qside, q * a_in, k * e_ex)
        lv = jnp.dot(x.astype(BF16), x.T.astype(BF16), preferred_element_type=F32)
        sc = sc + lv * mlow_ref[(HG_LOW_LEVELS if rev else 0) + lvl]
        tot3 = tot.reshape(HG_TILES, SUBLANES, HEAD_W)
        prev = pltpu.roll(tot3, h, 1).reshape(c, HEAD_W)
        nxt = pltpu.roll(tot3, SUBLANES - h, 1).reshape(c, HEAD_W)
        if rev:
            a_in = a_in * jnp.where(bit, 1.0, nxt)
            e_ex = e_ex * jnp.where(bit, prev, 1.0)
        else:
            a_in = a_in * jnp.where(bit, prev, 1.0)
            e_ex = e_ex * jnp.where(bit, 1.0, nxt)
        tot = tot * jnp.where(bit, prev, nxt)

    tiles = lambda arr: [arr[i * SUBLANES:(i + 1) * SUBLANES] for i in range(HG_TILES)]
    q_t, k_t, a_t, e_t, sc_t = tiles(q), tiles(k), tiles(a_in), tiles(e_ex), tiles(sc)
    tb = tiles(tot)
    for lvl in range(HG_LOW_LEVELS, HG_LEVELS):
        ht = (1 << lvl) // SUBLANES
        n_blocks = HG_TILES // (2 * ht)
        x_t = [None] * HG_TILES
        q_idx = []
        for b in range(n_blocks):
            lo = range(2 * b * ht, (2 * b + 1) * ht)
            hi = range((2 * b + 1) * ht, (2 * b + 2) * ht)
            q_half, k_half = (lo, hi) if rev else (hi, lo)
            for i in q_half:
                x_t[i] = q_t[i] * a_t[i]
            for i in k_half:
                x_t[i] = k_t[i] * e_t[i]
            q_idx.extend(q_half)
        x = jnp.concatenate(x_t, axis=0)
        xq = jnp.concatenate([x_t[i] for i in q_idx], axis=0).astype(BF16)
        lv = jnp.dot(xq, x.T.astype(BF16), preferred_element_type=F32)
        mk = mhigh_ref[(HG_LEVELS - HG_LOW_LEVELS if rev else 0) + lvl - HG_LOW_LEVELS]
        for n, i in enumerate(q_idx):
            rows = slice(n * SUBLANES, (n + 1) * SUBLANES)
            sc_t[i] = sc_t[i] + lv[rows] * mk[rows]
        new_tb = []
        for b in range(n_blocks):
            t_lo, t_hi = tb[2 * b], tb[2 * b + 1]
            lo = range(2 * b * ht, (2 * b + 1) * ht)
            hi = range((2 * b + 1) * ht, (2 * b + 2) * ht)
            if rev:
                for i in lo:
                    a_t[i] = a_t[i] * t_hi
                for i in hi:
                    e_t[i] = e_t[i] * t_lo
            else:
                for i in hi:
                    a_t[i] = a_t[i] * t_lo
                for i in lo:
                    e_t[i] = e_t[i] * t_hi
            new_tb.append(t_lo * t_hi)
        tb = new_tb

    sc = jnp.concatenate(sc_t, axis=0)
    qa = jnp.concatenate([q_t[i] * a_t[i] for i in range(HG_TILES)], axis=0)
    ke = jnp.concatenate([k_t[i] * e_t[i] for i in range(HG_TILES)], axis=0)
    vb = v.astype(BF16)
    intra = jnp.dot(sc.astype(BF16), vb, preferred_element_type=F32)
    inter = lax.dot_general(qa.astype(BF16), st_t.astype(BF16), NT_DIMS,
                            preferred_element_type=F32)
    upd = jnp.dot(v.T.astype(BF16), ke.astype(BF16), preferred_element_type=F32)
    st_t = st_t * tb[0][0:1, :] + upd
    return inter + intra, st_t


def _mixer_kernel(lam_ref, q_ref, k_ref, vt_ref, ag_ref, hq_ref, zf_ref, zb_ref, hv_ref, hg_ref,
                  lb_ref, hgain_ref, mlow_ref, mhigh_ref, ao_ref, ho_ref,
                  s_scr, acc_scr, fin_scr, of_scr, ob_scr, st_scr, *, tq, tk, tr, out_scale):
    seq = q_ref.shape[1]
    c = HG_CHUNK
    n = seq // c
    per_tile = n // (seq // tq)
    lb_f = lb_ref[0, 0:1, :]
    lb_b = lb_ref[0, 1:2, :]
    st_scr[...] = jnp.zeros(st_scr.shape, F32)

    def scans(u):
        st_f = st_scr[0]
        st_b = st_scr[1]
        for r in range(per_tile):
            i = u * per_tile + r
            cf = pl.multiple_of(i * c, c)
            cb = pl.multiple_of((n - 1 - i) * c, c)
            o_f, st_f = _hg_chunk(hq_ref[0, pl.ds(cf, c), :], zf_ref[0, pl.ds(cf, c), :],
                                  hv_ref[0, pl.ds(cf, c), :], st_f, lb_f, mlow_ref, mhigh_ref,
                                  False)
            of_scr[pl.ds(cf, c), :] = o_f
            o_b, st_b = _hg_chunk(hq_ref[0, pl.ds(cb, c), :], zb_ref[0, pl.ds(cb, c), :],
                                  hv_ref[0, pl.ds(cb, c), :], st_b, lb_b, mlow_ref, mhigh_ref,
                                  True)
            ob_scr[pl.ds(cb, c), :] = o_b
        st_scr[0] = st_f
        st_scr[1] = st_b

    _attn_body(lam_ref, q_ref, k_ref, vt_ref, ag_ref, ao_ref, s_scr, acc_scr, fin_scr,
               tq=tq, tk=tk, out_scale=out_scale, side_work=scans)

    gain = hgain_ref[0]

    def finish(i, carry):
        off = pl.multiple_of(i * tr, tr)
        o = of_scr[pl.ds(off, tr), :] + ob_scr[pl.ds(off, tr), :]
        y = _rms_scale(o, gain) * _sigmoid(hg_ref[0, pl.ds(off, tr), :])
        ho_ref[0, pl.ds(off, tr), :] = y.astype(ho_ref.dtype)
        return carry

    lax.fori_loop(0, seq // tr, finish, 0)


def _mixers(qk, vt, hg, lam, lbs, da_gains, hg_gains, layer, batch, seq, tq, tk, tr, out_scale):
    qk3 = qk.reshape(batch, seq, 2 * MIX_W)
    hg3 = hg.reshape(batch, seq, HG_W)
    nk = seq // tk
    assert nk % 2 == 0, "the attention pipeline walks key chunks in pairs"
    assert (seq // HG_CHUNK) % (seq // tq) == 0, "whole scan chunks per query tile"
    mlow, mhigh = (jnp.asarray(a) for a in _hg_masks())
    col = lambda g: (lambda b, h: (b, 0, g * N_HEADS + h))
    head_gain = pl.BlockSpec((None, 1, 1, HEAD_W), lambda b, h: (layer, h, 0, 0))
    seq_block = lambda g: pl.BlockSpec((1, seq, HEAD_W), col(g))
    out = jax.ShapeDtypeStruct((batch, seq, MIX_W), BF16)
    kern = functools.partial(_mixer_kernel, tq=tq, tk=tk, tr=tr, out_scale=out_scale)
    return pl.pallas_call(
        kern,
        out_shape=(out, out),
        grid=(batch, N_HEADS),
        in_specs=[pl.BlockSpec(memory_space=pltpu.SMEM),
                  seq_block(0),
                  seq_block(1),
                  pl.BlockSpec((nk, HEAD_W, tk), lambda b, h: (b, h, 0)),
                  head_gain,
                  seq_block(0), seq_block(1), seq_block(2), seq_block(3), seq_block(4),
                  pl.BlockSpec((1, 2, HEAD_W), lambda b, h: (h, 0, 0)),
                  head_gain,
                  pl.BlockSpec(mlow.shape, lambda b, h: (0, 0, 0)),
                  pl.BlockSpec(mhigh.shape, lambda b, h: (0, 0, 0))],
        out_specs=(pl.BlockSpec((1, seq, HEAD_W), lambda b, h: (b, 0, h)),
                   pl.BlockSpec((1, seq, HEAD_W), lambda b, h: (b, 0, h))),
        scratch_shapes=[pltpu.VMEM((2, tk, 2 * tq), F32),
                        pltpu.VMEM((HEAD_W + ONES_ROWS, 2 * tq), F32),
                        pltpu.VMEM((HEAD_W + ONES_ROWS, 2 * tq), F32),
                        pltpu.VMEM((seq, HEAD_W), F32),
                        pltpu.VMEM((seq, HEAD_W), F32),
                        pltpu.VMEM((2, HEAD_W, HEAD_W), F32)],
        compiler_params=_cparams(("parallel", "parallel")),
        name="mixers",
    )(lam, qk3, qk3, vt, da_gains, hg3, hg3, hg3, hg3, hg3, lbs, hg_gains, mlow, mhigh)


def _post_kernel(a_ref, b_ref, gate_ref, x_ref, wa_ref, wb_ref, wo_ref, wg_ref, wu_ref, wd_ref,
                 g1_ref, g2_ref, *out_refs, emit_x):
    act_scr = out_refs[-1]
    ya = jnp.dot(a_ref[...], wa_ref[...].astype(BF16), preferred_element_type=F32)
    yb = jnp.dot(b_ref[...], wb_ref[...].astype(BF16), preferred_element_type=F32)
    merged = (gate_ref[:, :D_MODEL].astype(F32) * ya
              + gate_ref[:, D_MODEL:].astype(F32) * yb).astype(BF16)
    x_mid = x_ref[...] + jnp.dot(merged, wo_ref[...].astype(BF16), preferred_element_type=F32)
    h = _rms_scale(x_mid, g1_ref[...]).astype(BF16)
    for c in range(FFN_HIDDEN // MXU_N):
        sl = slice(c * MXU_N, (c + 1) * MXU_N)
        gate = jnp.dot(h, wg_ref[:, sl], preferred_element_type=F32)
        up = jnp.dot(h, wu_ref[:, sl], preferred_element_type=F32)
        act_scr[:, sl] = (gate * _sigmoid(gate) * up).astype(BF16)
    x_new = x_mid + jnp.dot(act_scr[...], wd_ref[...], preferred_element_type=F32)
    normed = _rms_scale(x_new, g2_ref[...])
    if emit_x:
        out_refs[0][...] = x_new
        out_refs[1][...] = normed.astype(out_refs[1].dtype)
    else:
        out_refs[0][...] = normed.astype(out_refs[0].dtype)


def _post(attn_o, hg_o, gates, x, wa, wb, wo, wg, wu, wd, layer, ffn_gains, next_gains,
          next_layer, tm, last):
    m = x.shape[0]
    row = lambda i: (i, 0)
    if last:
        out_shape = jax.ShapeDtypeStruct((m, D_MODEL), F32)
        out_specs = pl.BlockSpec((tm, D_MODEL), row)
    else:
        out_shape = (jax.ShapeDtypeStruct((m, D_MODEL), F32),
                     jax.ShapeDtypeStruct((m, D_MODEL), BF16))
        out_specs = (pl.BlockSpec((tm, D_MODEL), row), pl.BlockSpec((tm, D_MODEL), row))
    return pl.pallas_call(
        functools.partial(_post_kernel, emit_x=not last),
        out_shape=out_shape,
        grid=(m // tm,),
        in_specs=[pl.BlockSpec((tm, MIX_W), row),
                  pl.BlockSpec((tm, MIX_W), row),
                  pl.BlockSpec((tm, GATE_W), row),
                  pl.BlockSpec((tm, D_MODEL), row),
                  _resident_spec((MIX_W, D_MODEL), layer),
                  _resident_spec((MIX_W, D_MODEL), layer),
                  _resident_spec((D_MODEL, D_MODEL), layer),
                  _resident_spec((D_MODEL, FFN_HIDDEN), layer),
                  _resident_spec((D_MODEL, FFN_HIDDEN), layer),
                  _resident_spec((FFN_HIDDEN, D_MODEL), layer),
                  _layer_spec((1, D_MODEL), layer),
                  _layer_spec((1, D_MODEL), next_layer)],
        out_specs=out_specs,
        scratch_shapes=[pltpu.VMEM((tm, FFN_HIDDEN), BF16)],
        compiler_params=_cparams(("parallel",)),
        name="post",
    )(attn_o, hg_o, gates, x, wa, wb, wo, wg, wu, wd, ffn_gains, next_gains)


def _rope_expand():
    half = ROT_DIM // 2
    e = np.zeros((2 * half, 3 * HEAD_W), np.float32)
    base = np.zeros((1, 3 * HEAD_W), np.float32)
    for l in range(HEAD_W):
        d = l % QK_DIM
        if d < half:
            e[d, l] = 1.0
            e[half + d, HEAD_W + l] = -1.0
        elif d < ROT_DIM:
            e[d - half, l] = 1.0
            e[half + d - half, 2 * HEAD_W + l] = 1.0
        else:
            base[0, l] = 1.0
    return e, base


def _rope_tables(positions):
    inv_freq = ROPE_THETA ** (-(jnp.arange(0, ROT_DIM, 2, dtype=F32) / ROT_DIM))
    ang = positions.astype(F32).reshape(-1, 1) * inv_freq
    cs = jnp.concatenate([jnp.cos(ang), jnp.sin(ang)], axis=1)
    def head8(v):
        c = v * (2.0 ** 16 + 1.0)
        return c - (c - v)

    hi = head8(cs)
    mid = head8(cs - hi)
    lo = cs - hi - mid
    e, base = _rope_expand()
    e3 = jnp.asarray(np.concatenate([e, e, e], axis=0), BF16)
    pieces = jnp.concatenate([hi, mid, lo], axis=1).astype(BF16)
    return jnp.dot(pieces, e3, preferred_element_type=F32) + jnp.asarray(base)


def kernel(x, positions, w_in, da_lambda, da_norm, hg_lb_logits, hg_norm, w_a, w_b, w_o,
           attn_norm, ffn_norm, w_gate, w_up, w_down, final_norm):
    batch, seq, _ = x.shape
    depth = w_in.shape[0]
    m = batch * seq
    tm = min(512, seq)
    tq = tk = tm

    rope_t = _rope_tables(positions)
    p = jax.nn.softmax(hg_lb_logits.astype(F32), axis=1)
    csum = jnp.cumsum(p, axis=1)
    lbs = csum - csum[:, :1]

    w_gate, w_up, w_down = (w.astype(BF16) for w in (w_gate, w_up, w_down))
    w_in, w_a, w_b, w_o = (w.astype(F32) for w in (w_in, w_a, w_b, w_o))
    attn_gains = attn_norm.astype(F32).reshape(depth, 1, D_MODEL)
    ffn_gains = ffn_norm.astype(F32).reshape(depth, 1, D_MODEL)
    final_gain = final_norm.astype(F32).reshape(1, 1, D_MODEL)
    da_gains = da_norm.astype(F32).reshape(depth, N_HEADS, 1, HEAD_W)
    hg_gains = hg_norm.astype(F32).reshape(depth, N_HEADS, 1, HEAD_W)

    xf = x.reshape(m, D_MODEL).astype(F32)
    h = None
    out = None
    for layer in range(depth):
        lam_init = 0.8 - 0.6 * math.exp(-0.3 * layer)
        l32 = da_lambda[layer].astype(F32)
        lam_full = (jnp.exp(jnp.sum(l32[0] * l32[1])) - jnp.exp(jnp.sum(l32[2] * l32[3]))
                    + lam_init)
        lam = lam_full.reshape(1).astype(F32)
        lb_layer = lbs[:, layer].reshape(2, N_HEADS, HEAD_W).transpose(1, 0, 2)

        if layer == 0:
            qk, vt, hg, gates = _in_proj(xf, w_in, layer, rope_t, tm, gains=attn_gains)
        else:
            qk, vt, hg, gates = _in_proj(h, w_in, layer, rope_t, tm)
        attn_o, hg_o = _mixers(qk, vt, hg, lam, lb_layer, da_gains, hg_gains, layer, batch, seq,
                               tq, tk, min(512, seq), 1.0 - lam_init)
        last = layer == depth - 1
        args = (attn_o.reshape(m, MIX_W), hg_o.reshape(m, MIX_W), gates, xf,
                w_a, w_b, w_o, w_gate, w_up, w_down, layer, ffn_gains)
        if last:
            out = _post(*args, final_gain, 0, tm, True)
        else:
            xf, h = _post(*args, attn_gains, layer + 1, tm, False)
    return out.reshape(batch, seq, D_MODEL).astype(x.dtype)
```
